```python
import math
import jax, jax.numpy as jnp
from jax import lax
import numpy as np

D_MODEL = 1024
BATCH = 8
SEQ = 2048
DEPTH = 1
DEC_BATCH = 2
DEC_SEQ = 8192
PAST_LEN = 128

N_META = 16
GRID_W = 64
MIX_WIDTH = D_MODEL
POOL_GROUPS = 4
POOL_WIDTH = MIX_WIDTH // 2
POOL_CH = POOL_WIDTH // POOL_GROUPS
POOL_WINDOWS = (2, 4, 8, 16)
NAT_WIDTH = MIX_WIDTH - POOL_WIDTH
NAT_HEADS = 8
NAT_HEAD_DIM = NAT_WIDTH // NAT_HEADS
NAT_KH = 8
NAT_KW = 16
IN_WIDTH = POOL_WIDTH + 3 * NAT_WIDTH
PEER_HEADS = 8
PEER_NKEYS = 128
PEER_EXPERTS = PEER_NKEYS * PEER_NKEYS
PEER_TOPK = 16
PEER_DK = 256
PEER_DK_HALF = PEER_DK // 2
PEER_CHUNK = 512
LN_EPS = 1e-5
NEG_INF = -1e30
DEEPNORM_ALPHA = float((2 * DEPTH) ** 0.25)
DEEPNORM_BETA = float((8 * DEPTH) ** -0.25)

kernel_name = "hybrid_pool_natten_peer_encoder"


def layer_norm(x, g, b):
    xf = x.astype(jnp.float32)
    mu = jnp.mean(xf, axis=-1, keepdims=True)
    var = jnp.mean(jnp.square(xf - mu), axis=-1, keepdims=True)
    y = (xf - mu) * lax.rsqrt(var + LN_EPS)
    return (y * g.astype(jnp.float32) + b.astype(jnp.float32)).astype(x.dtype)


def pool_mixer(z, pool_w, pool_scale):
    B, L, _ = z.shape
    zg = z.reshape(B, L, POOL_GROUPS, POOL_CH)
    zf = zg.astype(jnp.float32)
    cs = jnp.concatenate([jnp.zeros((B, 1, POOL_GROUPS, POOL_CH), jnp.float32),
                          jnp.cumsum(zf, axis=1)], axis=1)
    t = jnp.arange(L)[:, None]
    w = jnp.array(POOL_WINDOWS, dtype=jnp.int32)[None, :]
    lo = jnp.clip(t - w // 2, 0, L - 1)
    hi = jnp.clip(t - w // 2 + w - 1, 0, L - 1)
    g_idx = jnp.arange(POOL_GROUPS)[None, :]
    tot = cs[:, hi + 1, g_idx] - cs[:, lo, g_idx]
    count = (hi - lo + 1).astype(jnp.float32)[None, :, :, None]
    pooled = (tot / count - zf).astype(z.dtype)
    y = jnp.einsum('blgc,gcd->blgd', pooled, pool_w).reshape(B, L, POOL_WIDTH)
    return y * pool_scale


def neighbourhood_attention(q, k, v, rpb):
    B, L, H, dh = q.shape
    T = L - N_META
    rows = T // GRID_W
    kh = min(NAT_KH, rows)
    scale = dh ** -0.5
    qm, km, vm = q[:, :N_META], k[:, :N_META], v[:, :N_META]
    qr = q[:, N_META:].reshape(B, rows, GRID_W, H, dh)
    kr = k[:, N_META:].reshape(B, rows, GRID_W, H, dh)
    vr = v[:, N_META:].reshape(B, rows, GRID_W, H, dh)
    r = jnp.arange(rows)
    r_start = jnp.clip(r - kh // 2, 0, rows - kh)
    row_idx = r_start[:, None] + jnp.arange(kh)[None, :]
    k_blk = kr[:, row_idx]
    v_blk = vr[:, row_idx]
    c = jnp.arange(GRID_W)
    c_start = jnp.clip(c - NAT_KW // 2, 0, GRID_W - NAT_KW)
    col_valid = (c[None, :] >= c_start[:, None]) & (c[None, :] < c_start[:, None] + NAT_KW)
    row_off = row_idx - r[:, None] + (NAT_KH - 1)
    col_off = jnp.clip(c[None, :] - c[:, None] + (NAT_KW - 1), 0, 2 * NAT_KW - 2)
    bias = rpb[:, row_off[:, None, :, None], col_off[None, :, None, :]]
    s_win = jnp.einsum('brqhd,brkwhd->bhrqkw', qr, k_blk).astype(jnp.float32) * scale
    s_win = s_win + bias.astype(jnp.float32)
    s_win = jnp.where(col_valid[:, None, :], s_win, NEG_INF)
    s_meta = jnp.einsum('brqhd,bmhd->bhrqm', qr, km).astype(jnp.float32) * scale
    s = jnp.concatenate([s_win.reshape(B, H, rows, GRID_W, kh * GRID_W), s_meta], axis=-1)
    p = jax.nn.softmax(s, axis=-1).astype(v.dtype)
    p_win = p[..., :kh * GRID_W].reshape(B, H, rows, GRID_W, kh, GRID_W)
    p_meta = p[..., kh * GRID_W:]
    o_r = (jnp.einsum('bhrqkw,brkwhd->brqhd', p_win, v_blk)
           + jnp.einsum('bhrqm,bmhd->brqhd', p_meta, vm)).reshape(B, T, H, dh)
    s_mm = jnp.einsum('bqhd,bkhd->bhqk', qm, km).astype(jnp.float32) * scale
    p_mm = jax.nn.softmax(s_mm, axis=-1).astype(v.dtype)
    o_m = jnp.einsum('bhqk,bkhd->bqhd', p_mm, vm)
    return jnp.concatenate([o_m, o_r], axis=1).reshape(B, L, H * dh)


def peer_ffn(h, wq, key1, key2, u_tab, v_tab):
    B, L, D = h.shape
    n = B * L
    pad = (-n) % PEER_CHUNK
    xt = jnp.pad(h.reshape(n, D), ((0, pad), (0, 0))).reshape(-1, PEER_CHUNK, D)

    def block(xc):
        q = (xc @ wq).reshape(PEER_CHUNK, PEER_HEADS, 2, PEER_DK_HALF)
        s1 = jnp.einsum('chk,hnk->chn', q[:, :, 0], key1).astype(jnp.float32)
        s2 = jnp.einsum('chk,hnk->chn', q[:, :, 1], key2).astype(jnp.float32)
        t1, i1 = lax.top_k(s1, PEER_TOPK)
        t2, i2 = lax.top_k(s2, PEER_TOPK)
        cand_s = (t1[..., :, None] + t2[..., None, :]).reshape(PEER_CHUNK, PEER_HEADS, PEER_TOPK * PEER_TOPK)
        cand_i = (i1[..., :, None] * PEER_NKEYS + i2[..., None, :]).reshape(PEER_CHUNK, PEER_HEADS, PEER_TOPK * PEER_TOPK)
        top_s, pos = lax.top_k(cand_s, PEER_TOPK)
        e_idx = jnp.take_along_axis(cand_i, pos, axis=-1).reshape(PEER_CHUNK, PEER_HEADS * PEER_TOPK)
        g = jax.nn.softmax(top_s, axis=-1).reshape(PEER_CHUNK, PEER_HEADS * PEER_TOPK)
        u_g = u_tab[e_idx]
        a = jnp.einsum('cd,ced->ce', xc, u_g).astype(jnp.float32)
        a = (jax.nn.gelu(a, approximate=False) * g).astype(xc.dtype)
        return jnp.einsum('ce,ced->cd', a, v_tab[e_idx])

    y = lax.map(block, xt).reshape(-1, D)[:n]
    return y.reshape(B, L, D)


def encode(x, meta_tokens, emb_ln_g, emb_ln_b, w_in, pool_w, pool_scale, nat_rpb, w_out,
           ln1_g, ln1_b, peer_wq, peer_key1, peer_key2, peer_u, peer_v, ln2_g, ln2_b):
    B = x.shape[0]
    meta = jnp.broadcast_to(meta_tokens[None], (B, N_META, D_MODEL)).astype(x.dtype)
    h = layer_norm(jnp.concatenate([meta, x], axis=1), emb_ln_g, emb_ln_b)
    L = h.shape[1]
    for i in range(DEPTH):
        z = h @ w_in[i]
        zp = z[..., :POOL_WIDTH]
        q = z[..., POOL_WIDTH:POOL_WIDTH + NAT_WIDTH].reshape(B, L, NAT_HEADS, NAT_HEAD_DIM)
        k = z[..., POOL_WIDTH + NAT_WIDTH:POOL_WIDTH + 2 * NAT_WIDTH].reshape(B, L, NAT_HEADS, NAT_HEAD_DIM)
        v = z[..., POOL_WIDTH + 2 * NAT_WIDTH:].reshape(B, L, NAT_HEADS, NAT_HEAD_DIM)
        a_out = pool_mixer(zp, pool_w[i], pool_scale[i])
        b_out = neighbourhood_attention(q, k, v, nat_rpb[i])
        mix = jnp.concatenate([a_out, b_out], axis=-1) @ w_out[i]
        h = layer_norm(DEEPNORM_ALPHA * h + mix, ln1_g[i], ln1_b[i])
        f = peer_ffn(h, peer_wq[i], peer_key1[i], peer_key2[i], peer_u[i], peer_v[i])
        h = layer_norm(DEEPNORM_ALPHA * h + f, ln2_g[i], ln2_b[i])
    return h[:, N_META:]


def setup_inputs(seed: int = 0) -> dict:
    key = jax.random.key(seed)
    ks = jax.random.split(key, 20)
    f32 = jnp.float32
    D = D_MODEL
    x_prompt = jax.random.normal(ks[0], (BATCH, SEQ, D), f32)
    x_sample = jax.random.normal(ks[1], (DEC_BATCH, DEC_SEQ, D), f32)
    meta_tokens = jax.random.normal(ks[2], (N_META, D), f32)
    emb_ln_g = 1.0 + 0.02 * jax.random.normal(ks[3], (D,), f32)
    emb_ln_b = 0.02 * jax.random.normal(ks[4], (D,), f32)
    col_scale = jnp.concatenate([jnp.ones((POOL_WIDTH + 2 * NAT_WIDTH,), f32),
                                 jnp.full((NAT_WIDTH,), DEEPNORM_BETA, f32)])
    w_in = jax.random.normal(ks[5], (DEPTH, D, IN_WIDTH), f32) * (D ** -0.5) * col_scale
    pool_w = jax.random.normal(ks[6], (DEPTH, POOL_GROUPS, POOL_CH, POOL_CH), f32) * (POOL_CH ** -0.5)
    pool_scale = 1.0 + 0.1 * jax.random.normal(ks[7], (DEPTH, POOL_WIDTH), f32)
    nat_rpb = 0.02 * jax.random.normal(ks[8], (DEPTH, NAT_HEADS, 2 * NAT_KH - 1, 2 * NAT_KW - 1), f32)
    w_out = jax.random.normal(ks[9], (DEPTH, MIX_WIDTH, D), f32) * (MIX_WIDTH ** -0.5) * DEEPNORM_BETA
    ln1_g = 1.0 + 0.02 * jax.random.normal(ks[10], (DEPTH, D), f32)
    ln1_b = 0.02 * jax.random.normal(ks[11], (DEPTH, D), f32)
    peer_wq = jax.random.normal(ks[12], (DEPTH, D, PEER_HEADS * PEER_DK), f32) * (D ** -0.5)
    peer_key1 = jax.random.normal(ks[13], (DEPTH, PEER_HEADS, PEER_NKEYS, PEER_DK_HALF), f32) * (PEER_DK_HALF ** -0.5)
    peer_key2 = jax.random.normal(ks[14], (DEPTH, PEER_HEADS, PEER_NKEYS, PEER_DK_HALF), f32) * (PEER_DK_HALF ** -0.5)
    peer_u = jax.random.normal(ks[15], (DEPTH, PEER_EXPERTS, D), f32) * (D ** -0.5)
    peer_v = jax.random.normal(ks[16], (DEPTH, PEER_EXPERTS, D), f32) * DEEPNORM_BETA * (PEER_HEADS ** -0.5)
    ln2_g = 1.0 + 0.02 * jax.random.normal(ks[17], (DEPTH, D), f32)
    ln2_b = 0.02 * jax.random.normal(ks[18], (DEPTH, D), f32)
    return {"x_prompt": x_prompt, "x_sample": x_sample, "meta_tokens": meta_tokens,
            "emb_ln_g": emb_ln_g, "emb_ln_b": emb_ln_b, "w_in": w_in, "pool_w": pool_w,
            "pool_scale": pool_scale, "nat_rpb": nat_rpb, "w_out": w_out,
            "ln1_g": ln1_g, "ln1_b": ln1_b, "peer_wq": peer_wq, "peer_key1": peer_key1,
            "peer_key2": peer_key2, "peer_u": peer_u, "peer_v": peer_v,
            "ln2_g": ln2_g, "ln2_b": ln2_b}


def reference(x_prompt, x_sample, meta_tokens, emb_ln_g, emb_ln_b, w_in, pool_w, pool_scale,
              nat_rpb, w_out, ln1_g, ln1_b, peer_wq, peer_key1, peer_key2, peer_u, peer_v,
              ln2_g, ln2_b):
    y_prompt = encode(x_prompt, meta_tokens, emb_ln_g, emb_ln_b, w_in, pool_w, pool_scale, nat_rpb,
                      w_out, ln1_g, ln1_b, peer_wq, peer_key1, peer_key2, peer_u, peer_v, ln2_g, ln2_b)
    y_sample = encode(x_sample, meta_tokens, emb_ln_g, emb_ln_b, w_in, pool_w, pool_scale, nat_rpb,
                      w_out, ln1_g, ln1_b, peer_wq, peer_key1, peer_key2, peer_u, peer_v, ln2_g, ln2_b)
    return (y_prompt, y_sample)
```

```python
import functools

import jax
import jax.numpy as jnp
from jax import lax
from jax.experimental import pallas as pl
from jax.experimental.pallas import tpu as pltpu

D_MODEL = 1024
N_META = 16
GRID_W = 64
POOL_GROUPS = 4
POOL_CH = 128
POOL_WIDTH = 512
POOL_WINDOWS = (2, 4, 8, 16)
NAT_WIDTH = 512
NAT_HEADS = 8
NAT_HEAD_DIM = 64
NAT_KH = 8
NAT_KW = 16
IN_WIDTH = 2048
PEER_HEADS = 8
PEER_NKEYS = 128
PEER_EXPERTS = PEER_NKEYS * PEER_NKEYS
PEER_TOPK = 16
PEER_DK_HALF = 128
N_PAIR = PEER_HEADS * PEER_TOPK
LN_EPS = 1e-5
NEG_INF = -1e30
DEPTH = 1
DEEPNORM_ALPHA = float((2 * DEPTH) ** 0.25)

LANES = 128
HALF_D = D_MODEL // 2
ROW_SUBLANES = HALF_D // LANES
VMEM_LIMIT = 56 * 1024 * 1024

TM_IN = 512
TQ = 512
TM_ROUTE = 256
TT = 128
SLOT_STRIDE = N_PAIR + 1

F32 = jnp.float32
BF16 = jnp.bfloat16


def _layer_norm(x, g, b):
    mu = jnp.mean(x, axis=-1, keepdims=True)
    xc = x - mu
    var = jnp.mean(xc * xc, axis=-1, keepdims=True)
    return xc * lax.rsqrt(var + LN_EPS) * g + b


def _dot_t(a, b):
    return lax.dot_general(a, b, (((1,), (1,)), ((), ())), preferred_element_type=F32)


def _const_spec(shape):
    nd = len(shape)
    return pl.BlockSpec(shape, lambda *_: (0,) * nd, pipeline_mode=pl.Buffered(1))


def _inproj_kernel(x_ref, g_ref, b_ref, w_ref, zp_ref, qkv_ref):
    h = _layer_norm(x_ref[...], g_ref[...], b_ref[...])
    z = jnp.dot(h.astype(BF16), w_ref[...], preferred_element_type=F32)
    zp_ref[...] = z[:, :POOL_WIDTH]
    qkv_ref[...] = z[:, POOL_WIDTH:].astype(BF16)


def _inproj(x, g, b, w_in_bf16, tm):
    n = x.shape[0]
    return pl.pallas_call(
        _inproj_kernel,
        grid=(n // tm,),
        in_specs=[
            pl.BlockSpec((tm, D_MODEL), lambda i: (i, 0)),
            _const_spec((1, D_MODEL)),
            _const_spec((1, D_MODEL)),
            _const_spec((D_MODEL, IN_WIDTH)),
        ],
        out_specs=[
            pl.BlockSpec((tm, POOL_WIDTH), lambda i: (i, 0)),
            pl.BlockSpec((tm, 3 * NAT_WIDTH), lambda i: (i, 0)),
        ],
        out_shape=[
            jax.ShapeDtypeStruct((n, POOL_WIDTH), F32),
            jax.ShapeDtypeStruct((n, 3 * NAT_WIDTH), BF16),
        ],
        compiler_params=pltpu.CompilerParams(
            dimension_semantics=("arbitrary",), vmem_limit_bytes=VMEM_LIMIT),
        name="inproj",
    )(x, g, b, w_in_bf16)


def _mixer_kernel(x_ref, zp_ref, zprev_ref, znext_ref, q_ref, k_ref, v_ref,
                  zmp_ref, km_ref, vm_ref, bias_ref, poolw_ref, pools_ref, wout_ref,
                  eg_ref, eb_ref, g1_ref, b1_ref, h1_ref, ext_ref, mix_ref, *, seq_len):
    i = pl.program_id(1)
    n_tiles = pl.num_programs(1)
    rows = seq_len // GRID_W
    scale = NAT_HEAD_DIM ** -0.5

    ext_ref[8:8 + TQ, :] = zp_ref[...]

    @pl.when(i == 0)
    def _():
        ext_ref[0:8, :] = zmp_ref[8:16, :]

    @pl.when(i > 0)
    def _():
        ext_ref[0:8, :] = zprev_ref[...]

    @pl.when(i == n_tiles - 1)
    def _():
        ext_ref[8 + TQ:16 + TQ, :] = jnp.zeros((8, POOL_WIDTH), F32)

    @pl.when(i < n_tiles - 1)
    def _():
        ext_ref[8 + TQ:16 + TQ, :] = znext_ref[...]

    t_seq = i * TQ + lax.broadcasted_iota(jnp.int32, (TQ, 1), 0)
    for g, w in enumerate(POOL_WINDOWS):
        cols = slice(g * POOL_CH, (g + 1) * POOL_CH)
        tot = ext_ref[8 - w // 2:8 - w // 2 + TQ, cols]
        for j in range(1, w):
            tot = tot + ext_ref[8 - w // 2 + j:8 - w // 2 + j + TQ, cols]
        over = jnp.maximum(t_seq + (w // 2 - 1) - (seq_len - 1), 0)
        count = (w - over).astype(F32)
        pooled = tot / count - ext_ref[8:8 + TQ, cols]
        y = jnp.dot(pooled.astype(BF16), poolw_ref[g], preferred_element_type=F32)
        mix_ref[:, cols] = (y * pools_ref[:, cols]).astype(BF16)

    lane = lax.broadcasted_iota(jnp.int32, (GRID_W, LANES), 1)
    first_head = lane < NAT_HEAD_DIM

    def row_body(rr, carry):
        r = i * (TQ // GRID_W) + rr
        rs = jnp.clip(r - NAT_KH // 2, 0, rows - NAT_KH)
        off = r - rs
        kstart = pl.multiple_of(rs * GRID_W, GRID_W)
        qstart = pl.multiple_of(rr * GRID_W, GRID_W)
        for p in range(NAT_HEADS // 2):
            cols = slice(p * LANES, (p + 1) * LANES)
            kp = k_ref[pl.ds(kstart, NAT_KH * GRID_W), cols]
            vp = v_ref[pl.ds(kstart, NAT_KH * GRID_W), cols]
            qp = q_ref[pl.ds(qstart, GRID_W), cols]
            kmp = km_ref[:, cols]
            vmp = vm_ref[:, cols]
            outs = []
            for hh in range(2):
                keep = first_head if hh == 0 else jnp.logical_not(first_head)
                qm = jnp.where(keep, qp, jnp.zeros_like(qp))
                s = _dot_t(qm, kp) * scale + bias_ref[off, 2 * p + hh]
                sm = _dot_t(qm, kmp) * scale
                m = jnp.maximum(jnp.max(s, axis=-1, keepdims=True),
                                jnp.max(sm, axis=-1, keepdims=True))
                e = jnp.exp(s - m)
                em = jnp.exp(sm - m)
                den = jnp.sum(e, axis=-1, keepdims=True) + jnp.sum(em, axis=-1, keepdims=True)
                pw = (e / den).astype(BF16)
                pm = (em / den).astype(BF16)
                outs.append(jnp.dot(pw, vp, preferred_element_type=F32)
                            + jnp.dot(pm, vmp, preferred_element_type=F32))
            o_pair = jnp.where(first_head, outs[0], outs[1])
            mix_ref[pl.ds(qstart, GRID_W), POOL_WIDTH + p * LANES:POOL_WIDTH + (p + 1) * LANES] = (
                o_pair.astype(BF16))
        return carry

    lax.fori_loop(0, TQ // GRID_W, row_body, 0)

    h0 = _layer_norm(x_ref[...], eg_ref[...], eb_ref[...])
    y = jnp.dot(mix_ref[...], wout_ref[...], preferred_element_type=F32)
    h1_ref[...] = _layer_norm(DEEPNORM_ALPHA * h0 + y, g1_ref[...], b1_ref[...])


def _mixer(x, zp, qkv, zm_pool, km, vm, bias, pool_w, pool_scale, w_out, eg, eb, g1, b1,
           batch, seq_len):
    n = batch * seq_len
    nt = seq_len // TQ
    hb = TQ // 8
    last_hb = n // 8 - 1
    kern = functools.partial(_mixer_kernel, seq_len=seq_len)
    return pl.pallas_call(
        kern,
        grid=(batch, nt),
        in_specs=[
            pl.BlockSpec((TQ, D_MODEL), lambda b, i: (b * nt + i, 0)),
            pl.BlockSpec((TQ, POOL_WIDTH), lambda b, i: (b * nt + i, 0)),
            pl.BlockSpec((8, POOL_WIDTH), lambda b, i: (jnp.maximum((b * nt + i) * hb - 1, 0), 0)),
            pl.BlockSpec((8, POOL_WIDTH), lambda b, i: (jnp.minimum((b * nt + i + 1) * hb, last_hb), 0)),
            pl.BlockSpec((TQ, NAT_WIDTH), lambda b, i: (b * nt + i, 0)),
            pl.BlockSpec((seq_len, NAT_WIDTH), lambda b, i: (b, 1), pipeline_mode=pl.Buffered(1)),
            pl.BlockSpec((seq_len, NAT_WIDTH), lambda b, i: (b, 2), pipeline_mode=pl.Buffered(1)),
            _const_spec((N_META, POOL_WIDTH)),
            _const_spec((N_META, NAT_WIDTH)),
            _const_spec((N_META, NAT_WIDTH)),
            _const_spec((NAT_KH, NAT_HEADS, GRID_W, NAT_KH * GRID_W)),
            _const_spec((POOL_GROUPS, POOL_CH, POOL_CH)),
            _const_spec((1, POOL_WIDTH)),
            _const_spec((D_MODEL, D_MODEL)),
            _const_spec((1, D_MODEL)),
            _const_spec((1, D_MODEL)),
            _const_spec((1, D_MODEL)),
            _const_spec((1, D_MODEL)),
        ],
        out_specs=pl.BlockSpec((TQ, D_MODEL), lambda b, i: (b * nt + i, 0)),
        out_shape=jax.ShapeDtypeStruct((n, D_MODEL), F32),
        scratch_shapes=[pltpu.VMEM((TQ + 16, POOL_WIDTH), F32),
                        pltpu.VMEM((TQ, D_MODEL), BF16)],
        compiler_params=pltpu.CompilerParams(
            dimension_semantics=("arbitrary", "arbitrary"), vmem_limit_bytes=VMEM_LIMIT),
        name="mixer",
    )(x, zp, zp, zp, qkv, qkv, qkv, zm_pool, km, vm, bias, pool_w, pool_scale, w_out,
      eg, eb, g1, b1)


def _attention_bias(rpb):
    c = jnp.arange(GRID_W)
    c_start = jnp.clip(c - NAT_KW // 2, 0, GRID_W - NAT_KW)
    col_valid = (c[None, :] >= c_start[:, None]) & (c[None, :] < c_start[:, None] + NAT_KW)
    col_off = jnp.clip(c[None, :] - c[:, None] + (NAT_KW - 1), 0, 2 * NAT_KW - 2)
    off = jnp.arange(NAT_KH)[:, None]
    j = jnp.arange(NAT_KH)[None, :]
    row_off = j - off + (NAT_KH - 1)
    b = rpb[:, row_off[:, :, None, None], col_off[None, None, :, :]]
    b = jnp.where(col_valid[None, None, None], b.astype(F32), NEG_INF)
    b = b.transpose(1, 0, 3, 2, 4)
    return b.reshape(NAT_KH, NAT_HEADS, GRID_W, NAT_KH * GRID_W)


def _top_rows(s, k):
    n_rows = s.shape[0]
    row = lax.broadcasted_iota(jnp.int32, s.shape, 0)
    vals, idxs = [], []
    for _ in range(k):
        m = jnp.max(s, axis=0, keepdims=True)
        idx = jnp.min(jnp.where(s == m, row, n_rows), axis=0, keepdims=True)
        vals.append(m)
        idxs.append(idx)
        s = jnp.where(row == idx, -jnp.inf, s)
    return jnp.concatenate(vals, axis=0), jnp.concatenate(idxs, axis=0)


def _route_kernel(h_ref, wq_ref, k1_ref, k2_ref, e_ref, g_ref):
    q = jnp.dot(h_ref[...].astype(BF16), wq_ref[...], preferred_element_type=F32)
    q = q.astype(BF16)
    rank = lax.broadcasted_iota(jnp.int32, (PEER_TOPK, LANES), 0)
    for c in range(TM_ROUTE // LANES):
        tok = slice(c * LANES, (c + 1) * LANES)
        e_rows, g_rows = [], []
        for h in range(PEER_HEADS):
            q1 = q[tok, (2 * h) * PEER_DK_HALF:(2 * h + 1) * PEER_DK_HALF]
            q2 = q[tok, (2 * h + 1) * PEER_DK_HALF:(2 * h + 2) * PEER_DK_HALF]
            t1, i1 = _top_rows(_dot_t(k1_ref[h], q1), PEER_TOPK)
            t2, i2 = _top_rows(_dot_t(k2_ref[h], q2), PEER_TOPK)
            cand = jnp.concatenate([t1[a:a + 1, :] + t2 for a in range(PEER_TOPK)], axis=0)
            top_s, pos = _top_rows(cand, PEER_TOPK)
            pa = pos >> 4
            pb = pos & (PEER_TOPK - 1)
            rows = []
            for j in range(PEER_TOPK):
                ea = jnp.sum(jnp.where(rank == pa[j:j + 1, :], i1, 0), axis=0, keepdims=True)
                eb = jnp.sum(jnp.where(rank == pb[j:j + 1, :], i2, 0), axis=0, keepdims=True)
                rows.append(ea * PEER_NKEYS + eb)
            e_rows.append(jnp.concatenate(rows, axis=0))
            ex = jnp.exp(top_s - top_s[0:1, :])
            g_rows.append(ex / jnp.sum(ex, axis=0, keepdims=True))
        e_all = jnp.concatenate(e_rows, axis=0) * ROW_SUBLANES
        e_ref[tok, :] = e_all.T
        g_ref[:, tok] = jnp.concatenate(g_rows, axis=0)


def _route(h1, wq_bf16, key1_bf16, key2_bf16):
    n = h1.shape[0]
    return pl.pallas_call(
        _route_kernel,
        grid=(n // TM_ROUTE,),
        in_specs=[
            pl.BlockSpec((TM_ROUTE, D_MODEL), lambda i: (i, 0)),
            _const_spec((D_MODEL, 2 * PEER_HEADS * PEER_DK_HALF)),
            _const_spec((PEER_HEADS, PEER_NKEYS, PEER_DK_HALF)),
            _const_spec((PEER_HEADS, PEER_NKEYS, PEER_DK_HALF)),
        ],
        out_specs=[
            pl.BlockSpec((TM_ROUTE, N_PAIR), lambda i: (i, 0)),
            pl.BlockSpec((N_PAIR, TM_ROUTE), lambda i: (0, i)),
        ],
        out_shape=[
            jax.ShapeDtypeStruct((n, N_PAIR), jnp.int32),
            jax.ShapeDtypeStruct((N_PAIR, n), F32),
        ],
        compiler_params=pltpu.CompilerParams(
            dimension_semantics=("arbitrary",), vmem_limit_bytes=VMEM_LIMIT),
        name="route",
    )(h1, wq_bf16, key1_bf16, key2_bf16)


def _pack_table(tab):
    b = lax.bitcast_convert_type(tab.astype(BF16), jnp.uint16).astype(jnp.uint32)
    w = b[:, :HALF_D] | (b[:, HALF_D:] << 16)
    return lax.bitcast_convert_type(w, jnp.int32).reshape(PEER_EXPERTS * ROW_SUBLANES, LANES)


def _unpack_row(w):
    lo = pltpu.bitcast(w << 16, F32)
    hi = pltpu.bitcast(w & jnp.int32(-65536), F32)
    return lo, hi


def _peer_u_kernel(e_ref, x_ref, u_ref, a_ref, slot_ref):
    lane = lax.broadcasted_iota(jnp.int32, (N_PAIR, TT), 1)

    def tok(t, acc):
        xt = x_ref[t]
        xlo = xt[0:ROW_SUBLANES, :]
        xhi = xt[ROW_SUBLANES:2 * ROW_SUBLANES, :]
        for k in range(N_PAIR):
            row = pl.multiple_of(e_ref[t, k], ROW_SUBLANES)
            lo, hi = _unpack_row(u_ref[pl.ds(row, ROW_SUBLANES), :])
            slot_ref[pl.ds(k, ROW_SUBLANES, stride=SLOT_STRIDE), :] = lo * xlo + hi * xhi
        part = slot_ref[0:N_PAIR, :]
        for j in range(1, ROW_SUBLANES):
            part = part + slot_ref[j * SLOT_STRIDE:j * SLOT_STRIDE + N_PAIR, :]
        a_col = jnp.sum(part, axis=-1, keepdims=True)
        return jnp.where(lane == t, a_col, acc)

    a_ref[...] = lax.fori_loop(0, TT, tok, jnp.zeros((N_PAIR, TT), F32))


def _peer_u(e_rows, h1, u_packed):
    n = h1.shape[0]
    return pl.pallas_call(
        _peer_u_kernel,
        grid=(n // TT,),
        in_specs=[
            pl.BlockSpec((TT, N_PAIR), lambda i: (i, 0), memory_space=pltpu.SMEM),
            pl.BlockSpec((TT, 8, LANES), lambda i: (i, 0, 0)),
            pl.BlockSpec(memory_space=pltpu.VMEM),
        ],
        out_specs=pl.BlockSpec((N_PAIR, TT), lambda i: (0, i)),
        out_shape=jax.ShapeDtypeStruct((N_PAIR, n), F32),
        scratch_shapes=[pltpu.VMEM((ROW_SUBLANES * SLOT_STRIDE + 8, LANES), F32)],
        compiler_params=pltpu.CompilerParams(
            dimension_semantics=("arbitrary",), vmem_limit_bytes=VMEM_LIMIT),
        name="peer_u",
    )(e_rows, h1.reshape(n, 8, LANES), u_packed)


def _peer_v_kernel(e_ref, a_ref, g_ref, v_ref, f_ref, w_ref, wrow_ref):
    a = a_ref[...]
    gelu = 0.5 * a * (1.0 + lax.erf(a * (2.0 ** -0.5)))
    w_ref[...] = gelu * g_ref[...]
    lane = lax.broadcasted_iota(jnp.int32, (N_PAIR, TT), 1)
    n_acc = 4

    def tok(t, carry):
        w_col = jnp.sum(jnp.where(lane == t, w_ref[...], 0.0), axis=-1, keepdims=True)
        wrow_ref[...] = jnp.broadcast_to(w_col, (N_PAIR, LANES))
        acc_lo = [jnp.zeros((ROW_SUBLANES, LANES), F32) for _ in range(n_acc)]
        acc_hi = [jnp.zeros((ROW_SUBLANES, LANES), F32) for _ in range(n_acc)]
        for k in range(N_PAIR):
            row = pl.multiple_of(e_ref[t, k], ROW_SUBLANES)
            lo, hi = _unpack_row(v_ref[pl.ds(row, ROW_SUBLANES), :])
            wk = jnp.broadcast_to(wrow_ref[k:k + 1, :], (ROW_SUBLANES, LANES))
            acc_lo[k % n_acc] = acc_lo[k % n_acc] + wk * lo
            acc_hi[k % n_acc] = acc_hi[k % n_acc] + wk * hi
        f_ref[t, 0:ROW_SUBLANES, :] = (acc_lo[0] + acc_lo[1]) + (acc_lo[2] + acc_lo[3])
        f_ref[t, ROW_SUBLANES:2 * ROW_SUBLANES, :] = (acc_hi[0] + acc_hi[1]) + (acc_hi[2] + acc_hi[3])
        return carry

    lax.fori_loop(0, TT, tok, 0)


def _peer_v(e_rows, a_t, g_t, v_packed):
    n = e_rows.shape[0]
    f = pl.pallas_call(
        _peer_v_kernel,
        grid=(n // TT,),
        in_specs=[
            pl.BlockSpec((TT, N_PAIR), lambda i: (i, 0), memory_space=pltpu.SMEM),
            pl.BlockSpec((N_PAIR, TT), lambda i: (0, i)),
            pl.BlockSpec((N_PAIR, TT), lambda i: (0, i)),
            pl.BlockSpec(memory_space=pltpu.VMEM),
        ],
        out_specs=pl.BlockSpec((TT, 8, LANES), lambda i: (i, 0, 0)),
        out_shape=jax.ShapeDtypeStruct((n, 8, LANES), F32),
        scratch_shapes=[pltpu.VMEM((N_PAIR, TT), F32),
                        pltpu.VMEM((N_PAIR, LANES), F32)],
        compiler_params=pltpu.CompilerParams(
            dimension_semantics=("arbitrary",), vmem_limit_bytes=VMEM_LIMIT),
        name="peer_v",
    )(e_rows, a_t, g_t, v_packed)
    return f.reshape(n, D_MODEL)


def _final_kernel(h_ref, f_ref, g_ref, b_ref, o_ref):
    o_ref[...] = _layer_norm(DEEPNORM_ALPHA * h_ref[...] + f_ref[...], g_ref[...], b_ref[...])


def _final(h1, f, g, b):
    n = h1.shape[0]
    return pl.pallas_call(
        _final_kernel,
        grid=(n // TM_IN,),
        in_specs=[
            pl.BlockSpec((TM_IN, D_MODEL), lambda i: (i, 0)),
            pl.BlockSpec((TM_IN, D_MODEL), lambda i: (i, 0)),
            _const_spec((1, D_MODEL)),
            _const_spec((1, D_MODEL)),
        ],
        out_specs=pl.BlockSpec((TM_IN, D_MODEL), lambda i: (i, 0)),
        out_shape=jax.ShapeDtypeStruct((n, D_MODEL), F32),
        compiler_params=pltpu.CompilerParams(
            dimension_semantics=("arbitrary",), vmem_limit_bytes=VMEM_LIMIT),
        name="final",
    )(h1, f, g, b)


def _encode(x, p):
    batch, seq_len, _ = x.shape
    assert seq_len % TQ == 0 and seq_len // GRID_W >= NAT_KH
    n = batch * seq_len
    x2 = x.reshape(n, D_MODEL)
    zp, qkv = _inproj(x2, p["eg"], p["eb"], p["w_in"], TM_IN)
    h1 = _mixer(x2, zp, qkv, p["zm_pool"], p["km"], p["vm"], p["bias"], p["pool_w"],
                p["pool_scale"], p["w_out"], p["eg"], p["eb"], p["g1"], p["b1"], batch, seq_len)
    e_rows, g_t = _route(h1, p["wq"], p["key1"], p["key2"])
    a_t = _peer_u(e_rows, h1, p["u"])
    f = _peer_v(e_rows, a_t, g_t, p["v"])
    y = _final(h1, f, p["g2"], p["b2"])
    return y.reshape(batch, seq_len, D_MODEL)


def kernel(x_prompt, x_sample, meta_tokens, emb_ln_g, emb_ln_b, w_in, pool_w, pool_scale, nat_rpb, w_out, ln1_g, ln1_b, peer_wq, peer_key1, peer_key2, peer_u, peer_v, ln2_g, ln2_b):
    row = lambda a: a.reshape(1, -1).astype(F32)
    p = {
        "eg": row(emb_ln_g), "eb": row(emb_ln_b),
        "w_in": w_in[0].astype(BF16),
        "pool_w": pool_w[0].astype(BF16),
        "pool_scale": row(pool_scale[0]),
        "bias": _attention_bias(nat_rpb[0]),
        "w_out": w_out[0].astype(BF16),
        "g1": row(ln1_g[0]), "b1": row(ln1_b[0]),
        "wq": peer_wq[0].astype(BF16),
        "key1": peer_key1[0].astype(BF16), "key2": peer_key2[0].astype(BF16),
        "u": _pack_table(peer_u[0]), "v": _pack_table(peer_v[0]),
        "g2": row(ln2_g[0]), "b2": row(ln2_b[0]),
    }
    zm_pool, qkv_m = _inproj(meta_tokens.astype(F32), p["eg"], p["eb"], p["w_in"], N_META)
    p["zm_pool"] = zm_pool
    p["km"] = qkv_m[:, NAT_WIDTH:2 * NAT_WIDTH]
    p["vm"] = qkv_m[:, 2 * NAT_WIDTH:]
    return (_encode(x_prompt, p), _encode(x_sample, p))
```

```python
import functools

import jax
import jax.numpy as jnp
from jax import lax
from jax.experimental import pallas as pl
from jax.experimental.pallas import tpu as pltpu

D_MODEL = 1024
N_META = 16
GRID_W = 64
POOL_GROUPS = 4
POOL_CH = 128
POOL_WIDTH = 512
POOL_WINDOWS = (2, 4, 8, 16)
NAT_WIDTH = 512
NAT_HEADS = 8
NAT_HEAD_DIM = 64
NAT_KH = 8
NAT_KW = 16
IN_WIDTH = 2048
PEER_HEADS = 8
PEER_NKEYS = 128
PEER_EXPERTS = PEER_NKEYS * PEER_NKEYS
PEER_TOPK = 16
PEER_DK_HALF = 128
N_PAIR = PEER_HEADS * PEER_TOPK
LN_EPS = 1e-5
NEG_INF = -1e30
DEPTH = 1
DEEPNORM_ALPHA = float((2 * DEPTH) ** 0.25)

LANES = 128
SUBLANES = 8
N_SLAB = D_MODEL // LANES
ROW_WORDS = N_SLAB // 2
VMEM_LIMIT = 56 * 1024 * 1024

TM_IN = 512
TQ = 512
TM_ROUTE = 256
TT = 128
TE = 512
GROUP = 8
GROUPS_PER_TILE = TT // GROUP
N_SLOT = 4
SLOT_STRIDE = N_PAIR + 1
SLOT_ROWS = N_SLAB * SLOT_STRIDE + SUBLANES

F32 = jnp.float32
BF16 = jnp.bfloat16


def _layer_norm(x, g, b):
    mu = jnp.mean(x, axis=-1, keepdims=True)
    xc = x - mu
    var = jnp.mean(xc * xc, axis=-1, keepdims=True)
    return xc * lax.rsqrt(var + LN_EPS) * g + b


def _dot_t(a, b):
    return lax.dot_general(a, b, (((1,), (1,)), ((), ())), preferred_element_type=F32)


def _const_spec(shape):
    nd = len(shape)
    return pl.BlockSpec(shape, lambda *_: (0,) * nd, pipeline_mode=pl.Buffered(1))


def _params(n_axes=1):
    return pltpu.CompilerParams(dimension_semantics=("arbitrary",) * n_axes,
                                vmem_limit_bytes=VMEM_LIMIT)


def _inproj_kernel(x_ref, g_ref, b_ref, w_ref, zp_ref, qkv_ref):
    h = _layer_norm(x_ref[...], g_ref[...], b_ref[...])
    z = jnp.dot(h.astype(BF16), w_ref[...], preferred_element_type=F32)
    zp_ref[...] = z[:, :POOL_WIDTH]
    qkv_ref[...] = z[:, POOL_WIDTH:].astype(BF16)


def _inproj(x, g, b, w_in_bf16, tm):
    n = x.shape[0]
    return pl.pallas_call(
        _inproj_kernel,
        grid=(n // tm,),
        in_specs=[
            pl.BlockSpec((tm, D_MODEL), lambda i: (i, 0)),
            _const_spec((1, D_MODEL)),
            _const_spec((1, D_MODEL)),
            _const_spec((D_MODEL, IN_WIDTH)),
        ],
        out_specs=[
            pl.BlockSpec((tm, POOL_WIDTH), lambda i: (i, 0)),
            pl.BlockSpec((tm, 3 * NAT_WIDTH), lambda i: (i, 0)),
        ],
        out_shape=[
            jax.ShapeDtypeStruct((n, POOL_WIDTH), F32),
            jax.ShapeDtypeStruct((n, 3 * NAT_WIDTH), BF16),
        ],
        compiler_params=_params(),
        name="inproj",
    )(x, g, b, w_in_bf16)


def _mixer_kernel(x_ref, zp_ref, zprev_ref, znext_ref, q_ref, k_ref, v_ref,
                  zmp_ref, km_ref, vm_ref, bias_ref, poolw_ref, pools_ref, wout_ref,
                  eg_ref, eb_ref, g1_ref, b1_ref, h1_ref, ext_ref, mix_ref, *, seq_len):
    i = pl.program_id(1)
    n_tiles = pl.num_programs(1)
    rows = seq_len // GRID_W
    scale = NAT_HEAD_DIM ** -0.5

    ext_ref[8:8 + TQ, :] = zp_ref[...]

    @pl.when(i == 0)
    def _():
        ext_ref[0:8, :] = zmp_ref[8:16, :]

    @pl.when(i > 0)
    def _():
        ext_ref[0:8, :] = zprev_ref[...]

    @pl.when(i == n_tiles - 1)
    def _():
        ext_ref[8 + TQ:16 + TQ, :] = jnp.zeros((8, POOL_WIDTH), F32)

    @pl.when(i < n_tiles - 1)
    def _():
        ext_ref[8 + TQ:16 + TQ, :] = znext_ref[...]

    t_seq = i * TQ + lax.broadcasted_iota(jnp.int32, (TQ, 1), 0)
    for g, w in enumerate(POOL_WINDOWS):
        cols = slice(g * POOL_CH, (g + 1) * POOL_CH)
        tot = ext_ref[8 - w // 2:8 - w // 2 + TQ, cols]
        for j in range(1, w):
            tot = tot + ext_ref[8 - w // 2 + j:8 - w // 2 + j + TQ, cols]
        over = jnp.maximum(t_seq + (w // 2 - 1) - (seq_len - 1), 0)
        count = (w - over).astype(F32)
        pooled = tot / count - ext_ref[8:8 + TQ, cols]
        y = jnp.dot(pooled.astype(BF16), poolw_ref[g], preferred_element_type=F32)
        mix_ref[:, cols] = (y * pools_ref[:, cols]).astype(BF16)

    lane = lax.broadcasted_iota(jnp.int32, (GRID_W, LANES), 1)
    first_head = lane < NAT_HEAD_DIM

    def row_body(rr, carry):
        r = i * (TQ // GRID_W) + rr
        rs = jnp.clip(r - NAT_KH // 2, 0, rows - NAT_KH)
        off = r - rs
        kstart = pl.multiple_of(rs * GRID_W, GRID_W)
        qstart = pl.multiple_of(rr * GRID_W, GRID_W)
        for p in range(NAT_HEADS // 2):
            cols = slice(p * LANES, (p + 1) * LANES)
            kp = k_ref[pl.ds(kstart, NAT_KH * GRID_W), cols]
            vp = v_ref[pl.ds(kstart, NAT_KH * GRID_W), cols]
            qp = q_ref[pl.ds(qstart, GRID_W), cols]
            kmp = km_ref[:, cols]
            vmp = vm_ref[:, cols]
            outs = []
            for hh in range(2):
                keep = first_head if hh == 0 else jnp.logical_not(first_head)
                qm = jnp.where(keep, qp, jnp.zeros_like(qp))
                s = _dot_t(qm, kp) * scale + bias_ref[off, 2 * p + hh]
                sm = _dot_t(qm, kmp) * scale
                m = jnp.maximum(jnp.max(s, axis=-1, keepdims=True),
                                jnp.max(sm, axis=-1, keepdims=True))
                e = jnp.exp(s - m)
                em = jnp.exp(sm - m)
                den = jnp.sum(e, axis=-1, keepdims=True) + jnp.sum(em, axis=-1, keepdims=True)
                pw = (e / den).astype(BF16)
                pm = (em / den).astype(BF16)
                outs.append(jnp.dot(pw, vp, preferred_element_type=F32)
                            + jnp.dot(pm, vmp, preferred_element_type=F32))
            o_pair = jnp.where(first_head, outs[0], outs[1])
            mix_ref[pl.ds(qstart, GRID_W), POOL_WIDTH + p * LANES:POOL_WIDTH + (p + 1) * LANES] = (
                o_pair.astype(BF16))
        return carry

    lax.fori_loop(0, TQ // GRID_W, row_body, 0)

    h0 = _layer_norm(x_ref[...], eg_ref[...], eb_ref[...])
    y = jnp.dot(mix_ref[...], wout_ref[...], preferred_element_type=F32)
    h1_ref[...] = _layer_norm(DEEPNORM_ALPHA * h0 + y, g1_ref[...], b1_ref[...])


def _mixer(x, zp, qkv, zm_pool, km, vm, bias, pool_w, pool_scale, w_out, eg, eb, g1, b1,
           batch, seq_len):
    n = batch * seq_len
    nt = seq_len // TQ
    hb = TQ // 8
    last_hb = n // 8 - 1
    kern = functools.partial(_mixer_kernel, seq_len=seq_len)
    return pl.pallas_call(
        kern,
        grid=(batch, nt),
        in_specs=[
            pl.BlockSpec((TQ, D_MODEL), lambda b, i: (b * nt + i, 0)),
            pl.BlockSpec((TQ, POOL_WIDTH), lambda b, i: (b * nt + i, 0)),
            pl.BlockSpec((8, POOL_WIDTH), lambda b, i: (jnp.maximum((b * nt + i) * hb - 1, 0), 0)),
            pl.BlockSpec((8, POOL_WIDTH), lambda b, i: (jnp.minimum((b * nt + i + 1) * hb, last_hb), 0)),
            pl.BlockSpec((TQ, NAT_WIDTH), lambda b, i: (b * nt + i, 0)),
            pl.BlockSpec((seq_len, NAT_WIDTH), lambda b, i: (b, 1), pipeline_mode=pl.Buffered(1)),
            pl.BlockSpec((seq_len, NAT_WIDTH), lambda b, i: (b, 2), pipeline_mode=pl.Buffered(1)),
            _const_spec((N_META, POOL_WIDTH)),
            _const_spec((N_META, NAT_WIDTH)),
            _const_spec((N_META, NAT_WIDTH)),
            _const_spec((NAT_KH, NAT_HEADS, GRID_W, NAT_KH * GRID_W)),
            _const_spec((POOL_GROUPS, POOL_CH, POOL_CH)),
            _const_spec((1, POOL_WIDTH)),
            _const_spec((D_MODEL, D_MODEL)),
            _const_spec((1, D_MODEL)),
            _const_spec((1, D_MODEL)),
            _const_spec((1, D_MODEL)),
            _const_spec((1, D_MODEL)),
        ],
        out_specs=pl.BlockSpec((TQ, D_MODEL), lambda b, i: (b * nt + i, 0)),
        out_shape=jax.ShapeDtypeStruct((n, D_MODEL), F32),
        scratch_shapes=[pltpu.VMEM((TQ + 16, POOL_WIDTH), F32),
                        pltpu.VMEM((TQ, D_MODEL), BF16)],
        compiler_params=_params(2),
        name="mixer",
    )(x, zp, zp, zp, qkv, qkv, qkv, zm_pool, km, vm, bias, pool_w, pool_scale, w_out,
      eg, eb, g1, b1)


def _attention_bias(rpb):
    c = jnp.arange(GRID_W)
    c_start = jnp.clip(c - NAT_KW // 2, 0, GRID_W - NAT_KW)
    col_valid = (c[None, :] >= c_start[:, None]) & (c[None, :] < c_start[:, None] + NAT_KW)
    col_off = jnp.clip(c[None, :] - c[:, None] + (NAT_KW - 1), 0, 2 * NAT_KW - 2)
    pick = (col_off[:, :, None] == jnp.arange(2 * NAT_KW - 1)).astype(F32)
    toep = jnp.einsum("qkc,hdc->hdqk", pick, rpb.astype(F32), precision=lax.Precision.HIGHEST)
    toep = jnp.where(col_valid[None, None], toep, NEG_INF)
    b = jnp.stack([toep[:, NAT_KH - 1 - off:2 * NAT_KH - 1 - off] for off in range(NAT_KH)])
    b = b.transpose(0, 1, 3, 2, 4)
    return b.reshape(NAT_KH, NAT_HEADS, GRID_W, NAT_KH * GRID_W)


def _top_rows(s, k, payload=None):
    n_rows = s.shape[0]
    row = lax.broadcasted_iota(jnp.int32, s.shape, 0)
    vals, picks = [], []
    for _ in range(k):
        m = jnp.max(s, axis=0, keepdims=True)
        idx = jnp.min(jnp.where(s == m, row, n_rows), axis=0, keepdims=True)
        hit = row == idx
        vals.append(m)
        picks.append(idx if payload is None
                     else jnp.max(jnp.where(hit, payload, -1), axis=0, keepdims=True))
        s = jnp.where(hit, -jnp.inf, s)
    return jnp.concatenate(vals, axis=0), jnp.concatenate(picks, axis=0)


def _candidates(first, second, combine):
    parts = [combine(first[0:1, :], second)]
    parts += [combine(first[a:a + 1, :], second[0:SUBLANES, :]) for a in range(1, PEER_TOPK)]
    return jnp.concatenate(parts, axis=0)


def _route_kernel(h_ref, wq_ref, k1_ref, k2_ref, e_ref, g_ref):
    q = jnp.dot(h_ref[...].astype(BF16), wq_ref[...], preferred_element_type=F32)
    q = q.astype(BF16)
    n_cand = PEER_TOPK + SUBLANES * (PEER_TOPK - 1)
    r = lax.broadcasted_iota(jnp.int32, (n_cand, LANES), 0)
    ca = jnp.where(r < PEER_TOPK, 0, ((r - PEER_TOPK) >> 3) + 1)
    cb = jnp.where(r < PEER_TOPK, r, (r - PEER_TOPK) & (SUBLANES - 1))
    reachable = (ca + 1) * (cb + 1) <= PEER_TOPK
    for c in range(TM_ROUTE // LANES):
        tok = slice(c * LANES, (c + 1) * LANES)
        e_rows, g_rows = [], []
        for h in range(PEER_HEADS):
            q1 = q[tok, (2 * h) * PEER_DK_HALF:(2 * h + 1) * PEER_DK_HALF]
            q2 = q[tok, (2 * h + 1) * PEER_DK_HALF:(2 * h + 2) * PEER_DK_HALF]
            t1, i1 = _top_rows(_dot_t(k1_ref[h], q1), PEER_TOPK)
            t2, i2 = _top_rows(_dot_t(k2_ref[h], q2), PEER_TOPK)
            cand = jnp.where(reachable, _candidates(t1, t2, lambda x, y: x + y), -jnp.inf)
            expert = _candidates(i1, i2, lambda x, y: x * PEER_NKEYS + y)
            top_s, top_e = _top_rows(cand, PEER_TOPK, payload=expert)
            e_rows.append(top_e)
            ex = jnp.exp(top_s - top_s[0:1, :])
            g_rows.append(ex / jnp.sum(ex, axis=0, keepdims=True))
        e_ref[tok, :] = (jnp.concatenate(e_rows, axis=0) * ROW_WORDS).T
        g_ref[tok, :] = jnp.concatenate(g_rows, axis=0).T


def _route(h1, wq_bf16, key1_bf16, key2_bf16):
    n = h1.shape[0]
    return pl.pallas_call(
        _route_kernel,
        grid=(n // TM_ROUTE,),
        in_specs=[
            pl.BlockSpec((TM_ROUTE, D_MODEL), lambda i: (i, 0)),
            _const_spec((D_MODEL, 2 * PEER_HEADS * PEER_DK_HALF)),
            _const_spec((PEER_HEADS, PEER_NKEYS, PEER_DK_HALF)),
            _const_spec((PEER_HEADS, PEER_NKEYS, PEER_DK_HALF)),
        ],
        out_specs=[
            pl.BlockSpec((TM_ROUTE, N_PAIR), lambda i: (i, 0)),
            pl.BlockSpec((TM_ROUTE, N_PAIR), lambda i: (i, 0)),
        ],
        out_shape=[
            jax.ShapeDtypeStruct((n, N_PAIR), jnp.int32),
            jax.ShapeDtypeStruct((n, N_PAIR), F32),
        ],
        compiler_params=_params(),
        name="route",
    )(h1, wq_bf16, key1_bf16, key2_bf16)


def _pack_kernel(t_ref, o_ref):
    o_ref[...] = pltpu.bitcast(t_ref[...].astype(BF16), jnp.int32)


def _pack_table(tab):
    out = pl.pallas_call(
        _pack_kernel,
        grid=(PEER_EXPERTS // TE,),
        in_specs=[pl.BlockSpec((TE, N_SLAB, LANES), lambda i: (i, 0, 0))],
        out_specs=pl.BlockSpec((TE, ROW_WORDS, LANES), lambda i: (i, 0, 0)),
        out_shape=jax.ShapeDtypeStruct((PEER_EXPERTS, ROW_WORDS, LANES), jnp.int32),
        compiler_params=_params(),
        name="pack_table",
    )(tab.reshape(PEER_EXPERTS, N_SLAB, LANES))
    return out.reshape(PEER_EXPERTS * ROW_WORDS, LANES)


def _split_rows(block, row):
    sub = lax.broadcasted_iota(jnp.int32, block.shape, 0)
    one = jnp.where(sub == row, block, 0.0)
    hi = one.astype(BF16)
    lo = (one - hi.astype(F32)).astype(BF16)
    return jnp.concatenate([hi, lo], axis=0)


def _fold(r):
    return r[0:SUBLANES] + r[SUBLANES:2 * SUBLANES]


def _stream_tile(e_hbm, tab_ref, idx, sems, slots, reduce_token, store_block, acc0, scale=None):
    i = pl.program_id(0)
    first = i * GROUPS_PER_TILE
    n_groups = pl.num_programs(0) * GROUPS_PER_TILE

    def fetch(group, b):
        return pltpu.make_async_copy(e_hbm.at[group], idx[b], sems.at[b])

    def gather(idx_ref, blk, g, slot):
        mult = None if scale is None else scale(blk, g)
        for k in range(N_PAIR):
            row = pl.multiple_of(idx_ref[g * N_PAIR + k], ROW_WORDS)
            vals = pltpu.bitcast(tab_ref[pl.ds(row, ROW_WORDS), :], BF16).astype(F32)
            slot[pl.ds(k, SUBLANES, stride=SLOT_STRIDE), :] = vals if mult is None else vals * mult

    def run_group(idx_ref, blk, prev_blk, acc_prev):
        acc = acc0
        for g in range(GROUP):
            cur, prev = slots[g % N_SLOT], slots[(g - 1) % N_SLOT]
            gather(idx_ref, blk, g, cur)
            if g == 0:
                store_block(prev_blk, reduce_token(prev, prev_blk, GROUP - 1, acc_prev))
            else:
                acc = reduce_token(prev, blk, g - 1, acc)
        return acc

    @pl.when(i == 0)
    def _():
        slots[-1][...] = jnp.zeros(slots[-1].shape, F32)
        fetch(first, 0).start()

    def pair(j, acc_b):
        blk_a = 2 * j
        blk_b = blk_a + 1
        fetch(first + blk_b, 1).start()
        fetch(first + blk_a, 0).wait()
        acc_a = run_group(idx[0], blk_a, jnp.maximum(blk_a - 1, 0), acc_b)

        @pl.when(first + blk_a + 2 < n_groups)
        def _():
            fetch(first + blk_a + 2, 0).start()

        fetch(first + blk_b, 1).wait()
        return run_group(idx[1], blk_b, blk_a, acc_a)

    acc_b = lax.fori_loop(0, GROUPS_PER_TILE // 2, pair, acc0)
    last = GROUPS_PER_TILE - 1
    store_block(last, reduce_token(slots[-1], last, GROUP - 1, acc_b))


def _peer_u_kernel(e_hbm, x_ref, u_ref, a_ref, at_ref, idx_a, idx_b, sems, *slots):
    chunk = 2 * SUBLANES
    lane = lax.broadcasted_iota(jnp.int32, (chunk, TT), 1)
    at_ref[...] = jnp.zeros(at_ref.shape, F32)

    def scale(blk, g):
        return x_ref[blk * GROUP + g]

    def reduce_token(slot, blk, row, acc):
        t = blk * GROUP + row
        for c in range(N_PAIR // chunk):
            part = slot[c * chunk:(c + 1) * chunk, :]
            for j in range(1, N_SLAB):
                part = part + slot[j * SLOT_STRIDE + c * chunk:j * SLOT_STRIDE + (c + 1) * chunk, :]
            a_col = jnp.sum(part, axis=-1, keepdims=True)
            rows = slice(c * chunk, (c + 1) * chunk)
            at_ref[rows, :] = jnp.where(lane == t, a_col, at_ref[rows, :])
        return acc

    def store_block(blk, acc):
        pass

    _stream_tile(e_hbm, u_ref, (idx_a, idx_b), sems, slots, reduce_token, store_block,
                 jnp.zeros((), jnp.int32), scale)
    a_ref[...] = at_ref[...].T


def _peer_v_kernel(e_hbm, a_ref, g_ref, v_ref, f_ref, w_ref, idx_a, idx_b, sems, *slots):
    a = a_ref[...]
    w_ref[...] = 0.5 * a * (1.0 + lax.erf(a * (2.0 ** -0.5))) * g_ref[...]

    def reduce_token(slot, blk, row, acc):
        r0 = pl.multiple_of(blk * GROUP, GROUP)
        lhs = _split_rows(w_ref[pl.ds(r0, GROUP), :], row)
        out = []
        for j in range(N_SLAB):
            slab = slot[j * SLOT_STRIDE:j * SLOT_STRIDE + N_PAIR, :].astype(BF16)
            out.append(acc[j] + _fold(jnp.dot(lhs, slab, preferred_element_type=F32)))
        return tuple(out)

    def store_block(blk, acc):
        r0 = pl.multiple_of(blk * GROUP, GROUP)
        for j in range(N_SLAB):
            f_ref[pl.ds(r0, GROUP), j * LANES:(j + 1) * LANES] = acc[j]

    _stream_tile(e_hbm, v_ref, (idx_a, idx_b), sems, slots, reduce_token, store_block,
                 tuple(jnp.zeros((GROUP, LANES), F32) for _ in range(N_SLAB)))


def _expert_scratch():
    return ([pltpu.SMEM((GROUP * N_PAIR,), jnp.int32),
             pltpu.SMEM((GROUP * N_PAIR,), jnp.int32),
             pltpu.SemaphoreType.DMA((2,))]
            + [pltpu.VMEM((SLOT_ROWS, LANES), F32) for _ in range(N_SLOT)])


def _peer_u(e_rows, h1, u_packed):
    n = h1.shape[0]
    return pl.pallas_call(
        _peer_u_kernel,
        grid=(n // TT,),
        in_specs=[
            pl.BlockSpec(memory_space=pl.ANY),
            pl.BlockSpec((TT, SUBLANES, LANES), lambda i: (i, 0, 0)),
            pl.BlockSpec(memory_space=pltpu.VMEM),
        ],
        out_specs=pl.BlockSpec((TT, N_PAIR), lambda i: (i, 0)),
        out_shape=jax.ShapeDtypeStruct((n, N_PAIR), F32),
        scratch_shapes=[pltpu.VMEM((N_PAIR, TT), F32)] + _expert_scratch(),
        compiler_params=_params(),
        name="peer_u",
    )(e_rows.reshape(n // GROUP, GROUP * N_PAIR), h1.reshape(n, SUBLANES, LANES), u_packed)


def _peer_v(e_rows, a, g, v_packed):
    n = a.shape[0]
    return pl.pallas_call(
        _peer_v_kernel,
        grid=(n // TT,),
        in_specs=[
            pl.BlockSpec(memory_space=pl.ANY),
            pl.BlockSpec((TT, N_PAIR), lambda i: (i, 0)),
            pl.BlockSpec((TT, N_PAIR), lambda i: (i, 0)),
            pl.BlockSpec(memory_space=pltpu.VMEM),
        ],
        out_specs=pl.BlockSpec((TT, D_MODEL), lambda i: (i, 0)),
        out_shape=jax.ShapeDtypeStruct((n, D_MODEL), F32),
        scratch_shapes=[pltpu.VMEM((TT, N_PAIR), F32)] + _expert_scratch(),
        compiler_params=_params(),
        name="peer_v",
    )(e_rows.reshape(n // GROUP, GROUP * N_PAIR), a, g, v_packed)


def _final_kernel(h_ref, f_ref, g_ref, b_ref, o_ref):
    o_ref[...] = _layer_norm(DEEPNORM_ALPHA * h_ref[...] + f_ref[...], g_ref[...], b_ref[...])


def _final(h1, f, g, b):
    n = h1.shape[0]
    return pl.pallas_call(
        _final_kernel,
        grid=(n // TM_IN,),
        in_specs=[
            pl.BlockSpec((TM_IN, D_MODEL), lambda i: (i, 0)),
            pl.BlockSpec((TM_IN, D_MODEL), lambda i: (i, 0)),
            _const_spec((1, D_MODEL)),
            _const_spec((1, D_MODEL)),
        ],
        out_specs=pl.BlockSpec((TM_IN, D_MODEL), lambda i: (i, 0)),
        out_shape=jax.ShapeDtypeStruct((n, D_MODEL), F32),
        compiler_params=_params(),
        name="final",
    )(h1, f, g, b)


def _encode(x, p):
    batch, seq_len, _ = x.shape
    assert seq_len % TQ == 0 and seq_len // GRID_W >= NAT_KH
    n = batch * seq_len
    x2 = x.reshape(n, D_MODEL)
    zp, qkv = _inproj(x2, p["eg"], p["eb"], p["w_in"], TM_IN)
    h1 = _mixer(x2, zp, qkv, p["zm_pool"], p["km"], p["vm"], p["bias"], p["pool_w"],
                p["pool_scale"], p["w_out"], p["eg"], p["eb"], p["g1"], p["b1"], batch, seq_len)
    e_rows, g = _route(h1, p["wq"], p["key1"], p["key2"])
    a = _peer_u(e_rows, h1, p["u"])
    f = _peer_v(e_rows, a, g, p["v"])
    y = _final(h1, f, p["g2"], p["b2"])
    return y.reshape(batch, seq_len, D_MODEL)


def kernel(x_prompt, x_sample, meta_tokens, emb_ln_g, emb_ln_b, w_in, pool_w, pool_scale, nat_rpb, w_out, ln1_g, ln1_b, peer_wq, peer_key1, peer_key2, peer_u, peer_v, ln2_g, ln2_b):
    row = lambda a: a.reshape(1, -1).astype(F32)
    p = {
        "eg": row(emb_ln_g), "eb": row(emb_ln_b),
        "w_in": w_in[0].astype(BF16),
        "pool_w": pool_w[0].astype(BF16),
        "pool_scale": row(pool_scale[0]),
        "bias": _attention_bias(nat_rpb[0]),
        "w_out": w_out[0].astype(BF16),
        "g1": row(ln1_g[0]), "b1": row(ln1_b[0]),
        "wq": peer_wq[0].astype(BF16),
        "key1": peer_key1[0].astype(BF16), "key2": peer_key2[0].astype(BF16),
        "u": _pack_table(peer_u[0].astype(F32)), "v": _pack_table(peer_v[0].astype(F32)),
        "g2": row(ln2_g[0]), "b2": row(ln2_b[0]),
    }
    zm_pool, qkv_m = _inproj(meta_tokens.astype(F32), p["eg"], p["eb"], p["w_in"], N_META)
    p["zm_pool"] = zm_pool
    p["km"] = qkv_m[:, NAT_WIDTH:2 * NAT_WIDTH]
    p["vm"] = qkv_m[:, 2 * NAT_WIDTH:]
    return (_encode(x_prompt, p), _encode(x_sample, p))
```

```python
import functools

import jax
import jax.numpy as jnp
from jax import lax
from jax.experimental import pallas as pl
from jax.experimental.pallas import tpu as pltpu

D_MODEL = 1024
N_META = 16
GRID_W = 64
POOL_GROUPS = 4
POOL_CH = 128
POOL_WIDTH = 512
POOL_WINDOWS = (2, 4, 8, 16)
NAT_WIDTH = 512
NAT_HEADS = 8
NAT_HEAD_DIM = 64
NAT_KH = 8
NAT_KW = 16
IN_WIDTH = 2048
PEER_HEADS = 8
PEER_NKEYS = 128
PEER_EXPERTS = PEER_NKEYS * PEER_NKEYS
PEER_TOPK = 16
PEER_DK_HALF = 128
N_PAIR = PEER_HEADS * PEER_TOPK
LN_EPS = 1e-5
NEG_INF = -1e30
DEPTH = 1
DEEPNORM_ALPHA = float((2 * DEPTH) ** 0.25)

LANES = 128
SUBLANES = 8
N_SLAB = D_MODEL // LANES
ROW_WORDS = N_SLAB // 2
VMEM_LIMIT = 56 * 1024 * 1024

TM_IN = 512
TQ = 512
TM_ROUTE = 256
TT = 128
TE = 512
GROUP = 8
IDX_BLOCKS = 2
IDX_TOKENS = IDX_BLOCKS * GROUP
IDX_PER_TILE = TT // IDX_TOKENS
N_SLOT = 4
SLOT_STRIDE = N_PAIR + 1
SLOT_ROWS = N_SLAB * SLOT_STRIDE + SUBLANES

F32 = jnp.float32
BF16 = jnp.bfloat16


def _layer_norm(x, g, b):
    mu = jnp.mean(x, axis=-1, keepdims=True)
    xc = x - mu
    var = jnp.mean(xc * xc, axis=-1, keepdims=True)
    return xc * lax.rsqrt(var + LN_EPS) * g + b


def _dot_t(a, b):
    return lax.dot_general(a, b, (((1,), (1,)), ((), ())), preferred_element_type=F32)


def _const_spec(shape):
    nd = len(shape)
    return pl.BlockSpec(shape, lambda *_: (0,) * nd, pipeline_mode=pl.Buffered(1))


def _params(n_axes=1):
    return pltpu.CompilerParams(dimension_semantics=("arbitrary",) * n_axes,
                                vmem_limit_bytes=VMEM_LIMIT)


def _inproj_kernel(x_ref, g_ref, b_ref, w_ref, zp_ref, qkv_ref):
    h = _layer_norm(x_ref[...], g_ref[...], b_ref[...])
    z = jnp.dot(h.astype(BF16), w_ref[...], preferred_element_type=F32)
    zp_ref[...] = z[:, :POOL_WIDTH]
    qkv_ref[...] = z[:, POOL_WIDTH:].astype(BF16)


def _inproj(x, g, b, w_in_bf16, tm):
    n = x.shape[0]
    return pl.pallas_call(
        _inproj_kernel,
        grid=(n // tm,),
        in_specs=[
            pl.BlockSpec((tm, D_MODEL), lambda i: (i, 0)),
            _const_spec((1, D_MODEL)),
            _const_spec((1, D_MODEL)),
            _const_spec((D_MODEL, IN_WIDTH)),
        ],
        out_specs=[
            pl.BlockSpec((tm, POOL_WIDTH), lambda i: (i, 0)),
            pl.BlockSpec((tm, 3 * NAT_WIDTH), lambda i: (i, 0)),
        ],
        out_shape=[
            jax.ShapeDtypeStruct((n, POOL_WIDTH), F32),
            jax.ShapeDtypeStruct((n, 3 * NAT_WIDTH), BF16),
        ],
        compiler_params=_params(),
        name="inproj",
    )(x, g, b, w_in_bf16)


def _mixer_kernel(x_ref, zp_ref, zprev_ref, znext_ref, q_ref, k_ref, v_ref,
                  zmp_ref, km_ref, vm_ref, bias_ref, poolw_ref, pools_ref, wout_ref,
                  eg_ref, eb_ref, g1_ref, b1_ref, h1_ref, ext_ref, mix_ref, *, seq_len):
    i = pl.program_id(1)
    n_tiles = pl.num_programs(1)
    rows = seq_len // GRID_W
    scale = NAT_HEAD_DIM ** -0.5

    ext_ref[8:8 + TQ, :] = zp_ref[...]

    @pl.when(i == 0)
    def _():
        ext_ref[0:8, :] = zmp_ref[8:16, :]

    @pl.when(i > 0)
    def _():
        ext_ref[0:8, :] = zprev_ref[...]

    @pl.when(i == n_tiles - 1)
    def _():
        ext_ref[8 + TQ:16 + TQ, :] = jnp.zeros((8, POOL_WIDTH), F32)

    @pl.when(i < n_tiles - 1)
    def _():
        ext_ref[8 + TQ:16 + TQ, :] = znext_ref[...]

    t_seq = i * TQ + lax.broadcasted_iota(jnp.int32, (TQ, 1), 0)
    for g, w in enumerate(POOL_WINDOWS):
        cols = slice(g * POOL_CH, (g + 1) * POOL_CH)
        tot = ext_ref[8 - w // 2:8 - w // 2 + TQ, cols]
        for j in range(1, w):
            tot = tot + ext_ref[8 - w // 2 + j:8 - w // 2 + j + TQ, cols]
        over = jnp.maximum(t_seq + (w // 2 - 1) - (seq_len - 1), 0)
        count = (w - over).astype(F32)
        pooled = tot / count - ext_ref[8:8 + TQ, cols]
        y = jnp.dot(pooled.astype(BF16), poolw_ref[g], preferred_element_type=F32)
        mix_ref[:, cols] = (y * pools_ref[:, cols]).astype(BF16)

    lane = lax.broadcasted_iota(jnp.int32, (GRID_W, LANES), 1)
    first_head = lane < NAT_HEAD_DIM

    def row_body(rr, carry):
        r = i * (TQ // GRID_W) + rr
        rs = jnp.clip(r - NAT_KH // 2, 0, rows - NAT_KH)
        off = r - rs
        kstart = pl.multiple_of(rs * GRID_W, GRID_W)
        qstart = pl.multiple_of(rr * GRID_W, GRID_W)
        for p in range(NAT_HEADS // 2):
            cols = slice(p * LANES, (p + 1) * LANES)
            kp = k_ref[pl.ds(kstart, NAT_KH * GRID_W), cols]
            vp = v_ref[pl.ds(kstart, NAT_KH * GRID_W), cols]
            qp = q_ref[pl.ds(qstart, GRID_W), cols]
            kmp = km_ref[:, cols]
            vmp = vm_ref[:, cols]
            outs = []
            for hh in range(2):
                keep = first_head if hh == 0 else jnp.logical_not(first_head)
                qm = jnp.where(keep, qp, jnp.zeros_like(qp))
                s = _dot_t(qm, kp) * scale + bias_ref[off, 2 * p + hh]
                sm = _dot_t(qm, kmp) * scale
                m = jnp.maximum(jnp.max(s, axis=-1, keepdims=True),
                                jnp.max(sm, axis=-1, keepdims=True))
                e = jnp.exp(s - m)
                em = jnp.exp(sm - m)
                den = jnp.sum(e, axis=-1, keepdims=True) + jnp.sum(em, axis=-1, keepdims=True)
                pw = (e / den).astype(BF16)
                pm = (em / den).astype(BF16)
                outs.append(jnp.dot(pw, vp, preferred_element_type=F32)
                            + jnp.dot(pm, vmp, preferred_element_type=F32))
            o_pair = jnp.where(first_head, outs[0], outs[1])
            mix_ref[pl.ds(qstart, GRID_W), POOL_WIDTH + p * LANES:POOL_WIDTH + (p + 1) * LANES] = (
                o_pair.astype(BF16))
        return carry

    lax.fori_loop(0, TQ // GRID_W, row_body, 0)

    h0 = _layer_norm(x_ref[...], eg_ref[...], eb_ref[...])
    y = jnp.dot(mix_ref[...], wout_ref[...], preferred_element_type=F32)
    h1_ref[...] = _layer_norm(DEEPNORM_ALPHA * h0 + y, g1_ref[...], b1_ref[...])


def _mixer(x, zp, qkv, zm_pool, km, vm, bias, pool_w, pool_scale, w_out, eg, eb, g1, b1,
           batch, seq_len):
    n = batch * seq_len
    nt = seq_len // TQ
    hb = TQ // 8
    last_hb = n // 8 - 1
    kern = functools.partial(_mixer_kernel, seq_len=seq_len)
    return pl.pallas_call(
        kern,
        grid=(batch, nt),
        in_specs=[
            pl.BlockSpec((TQ, D_MODEL), lambda b, i: (b * nt + i, 0)),
            pl.BlockSpec((TQ, POOL_WIDTH), lambda b, i: (b * nt + i, 0)),
            pl.BlockSpec((8, POOL_WIDTH), lambda b, i: (jnp.maximum((b * nt + i) * hb - 1, 0), 0)),
            pl.BlockSpec((8, POOL_WIDTH), lambda b, i: (jnp.minimum((b * nt + i + 1) * hb, last_hb), 0)),
            pl.BlockSpec((TQ, NAT_WIDTH), lambda b, i: (b * nt + i, 0)),
            pl.BlockSpec((seq_len, NAT_WIDTH), lambda b, i: (b, 1), pipeline_mode=pl.Buffered(1)),
            pl.BlockSpec((seq_len, NAT_WIDTH), lambda b, i: (b, 2), pipeline_mode=pl.Buffered(1)),
            _const_spec((N_META, POOL_WIDTH)),
            _const_spec((N_META, NAT_WIDTH)),
            _const_spec((N_META, NAT_WIDTH)),
            _const_spec((NAT_KH, NAT_HEADS, GRID_W, NAT_KH * GRID_W)),
            _const_spec((POOL_GROUPS, POOL_CH, POOL_CH)),
            _const_spec((1, POOL_WIDTH)),
            _const_spec((D_MODEL, D_MODEL)),
            _const_spec((1, D_MODEL)),
            _const_spec((1, D_MODEL)),
            _const_spec((1, D_MODEL)),
            _const_spec((1, D_MODEL)),
        ],
        out_specs=pl.BlockSpec((TQ, D_MODEL), lambda b, i: (b * nt + i, 0)),
        out_shape=jax.ShapeDtypeStruct((n, D_MODEL), F32),
        scratch_shapes=[pltpu.VMEM((TQ + 16, POOL_WIDTH), F32),
                        pltpu.VMEM((TQ, D_MODEL), BF16)],
        compiler_params=_params(2),
        name="mixer",
    )(x, zp, zp, zp, qkv, qkv, qkv, zm_pool, km, vm, bias, pool_w, pool_scale, w_out,
      eg, eb, g1, b1)


def _attention_bias(rpb):
    c = jnp.arange(GRID_W)
    c_start = jnp.clip(c - NAT_KW // 2, 0, GRID_W - NAT_KW)
    col_valid = (c[None, :] >= c_start[:, None]) & (c[None, :] < c_start[:, None] + NAT_KW)
    col_off = jnp.clip(c[None, :] - c[:, None] + (NAT_KW - 1), 0, 2 * NAT_KW - 2)
    pick = (col_off[:, :, None] == jnp.arange(2 * NAT_KW - 1)).astype(F32)
    toep = jnp.einsum("qkc,hdc->hdqk", pick, rpb.astype(F32), precision=lax.Precision.HIGHEST)
    toep = jnp.where(col_valid[None, None], toep, NEG_INF)
    b = jnp.stack([toep[:, NAT_KH - 1 - off:2 * NAT_KH - 1 - off] for off in range(NAT_KH)])
    b = b.transpose(0, 1, 3, 2, 4)
    return b.reshape(NAT_KH, NAT_HEADS, GRID_W, NAT_KH * GRID_W)


def _top_rows(s, k, payload=None):
    n_rows = s.shape[0]
    row = lax.broadcasted_iota(jnp.int32, s.shape, 0).astype(F32)
    vals, picks = [], []
    for _ in range(k):
        m = jnp.max(s, axis=0, keepdims=True)
        idx = jnp.min(jnp.where(s == m, row, float(n_rows)), axis=0, keepdims=True)
        hit = row == idx
        vals.append(m)
        picks.append(idx if payload is None
                     else jnp.max(jnp.where(hit, payload, -1.0), axis=0, keepdims=True))
        s = jnp.where(hit, -jnp.inf, s)
    return jnp.concatenate(vals, axis=0), jnp.concatenate(picks, axis=0)


def _candidates(first, second, combine):
    parts = [combine(first[0:1, :], second)]
    parts += [combine(first[a:a + 1, :], second[0:SUBLANES, :]) for a in range(1, PEER_TOPK)]
    return jnp.concatenate(parts, axis=0)


def _route_kernel(h_ref, wq_ref, k1_ref, k2_ref, e_ref, g_ref):
    q = jnp.dot(h_ref[...].astype(BF16), wq_ref[...], preferred_element_type=F32)
    q = q.astype(BF16)
    n_cand = PEER_TOPK + SUBLANES * (PEER_TOPK - 1)
    r = lax.broadcasted_iota(jnp.int32, (n_cand, LANES), 0)
    ca = jnp.where(r < PEER_TOPK, 0, ((r - PEER_TOPK) >> 3) + 1)
    cb = jnp.where(r < PEER_TOPK, r, (r - PEER_TOPK) & (SUBLANES - 1))
    reachable = (ca + 1) * (cb + 1) <= PEER_TOPK
    for c in range(TM_ROUTE // LANES):
        tok = slice(c * LANES, (c + 1) * LANES)
        e_rows, g_rows = [], []
        for h in range(PEER_HEADS):
            q1 = q[tok, (2 * h) * PEER_DK_HALF:(2 * h + 1) * PEER_DK_HALF]
            q2 = q[tok, (2 * h + 1) * PEER_DK_HALF:(2 * h + 2) * PEER_DK_HALF]
            t1, i1 = _top_rows(_dot_t(k1_ref[h], q1), PEER_TOPK)
            t2, i2 = _top_rows(_dot_t(k2_ref[h], q2), PEER_TOPK)
            cand = jnp.where(reachable, _candidates(t1, t2, lambda x, y: x + y), -jnp.inf)
            expert = _candidates(i1, i2, lambda x, y: x * PEER_NKEYS + y)
            top_s, top_e = _top_rows(cand, PEER_TOPK, payload=expert)
            e_rows.append(top_e)
            ex = jnp.exp(top_s - top_s[0:1, :])
            g_rows.append(ex / jnp.sum(ex, axis=0, keepdims=True))
        e_all = jnp.concatenate(e_rows, axis=0).astype(jnp.int32)
        e_ref[tok, :] = (e_all * ROW_WORDS).T
        g_ref[tok, :] = jnp.concatenate(g_rows, axis=0).T


def _route(h1, wq_bf16, key1_bf16, key2_bf16):
    n = h1.shape[0]
    return pl.pallas_call(
        _route_kernel,
        grid=(n // TM_ROUTE,),
        in_specs=[
            pl.BlockSpec((TM_ROUTE, D_MODEL), lambda i: (i, 0)),
            _const_spec((D_MODEL, 2 * PEER_HEADS * PEER_DK_HALF)),
            _const_spec((PEER_HEADS, PEER_NKEYS, PEER_DK_HALF)),
            _const_spec((PEER_HEADS, PEER_NKEYS, PEER_DK_HALF)),
        ],
        out_specs=[
            pl.BlockSpec((TM_ROUTE, N_PAIR), lambda i: (i, 0)),
            pl.BlockSpec((TM_ROUTE, N_PAIR), lambda i: (i, 0)),
        ],
        out_shape=[
            jax.ShapeDtypeStruct((n, N_PAIR), jnp.int32),
            jax.ShapeDtypeStruct((n, N_PAIR), F32),
        ],
        compiler_params=_params(),
        name="route",
    )(h1, wq_bf16, key1_bf16, key2_bf16)


def _pack_kernel(t_ref, o_ref):
    o_ref[...] = pltpu.bitcast(t_ref[...].astype(BF16), jnp.int32)


def _pack_table(tab):
    out = pl.pallas_call(
        _pack_kernel,
        grid=(PEER_EXPERTS // TE,),
        in_specs=[pl.BlockSpec((TE, N_SLAB, LANES), lambda i: (i, 0, 0))],
        out_specs=pl.BlockSpec((TE, ROW_WORDS, LANES), lambda i: (i, 0, 0)),
        out_shape=jax.ShapeDtypeStruct((PEER_EXPERTS, ROW_WORDS, LANES), jnp.int32),
        compiler_params=_params(),
        name="pack_table",
    )(tab.reshape(PEER_EXPERTS, N_SLAB, LANES))
    return out.reshape(PEER_EXPERTS * ROW_WORDS, LANES)


def _split_rows(block, row):
    sub = lax.broadcasted_iota(jnp.int32, block.shape, 0)
    one = jnp.where(sub == row, block, 0.0)
    hi = one.astype(BF16)
    lo = (one - hi.astype(F32)).astype(BF16)
    return jnp.concatenate([hi, lo], axis=0)


def _fold(r):
    return r[0:SUBLANES] + r[SUBLANES:2 * SUBLANES]


def _stream_tile(e_hbm, tab_ref, idx, sems, slots, reduce_token, store_block, acc0, scale=None):
    i = pl.program_id(0)
    first = i * IDX_PER_TILE
    n_idx = pl.num_programs(0) * IDX_PER_TILE

    def fetch(group, b):
        return pltpu.make_async_copy(e_hbm.at[group], idx[b], sems.at[b])

    def gather(idx_ref, tok, blk, g, slot):
        mult = None if scale is None else scale(blk, g)
        for k in range(N_PAIR):
            row = pl.multiple_of(idx_ref[tok * N_PAIR + k], ROW_WORDS)
            vals = pltpu.bitcast(tab_ref[pl.ds(row, ROW_WORDS), :], BF16).astype(F32)
            slot[pl.ds(k, SUBLANES, stride=SLOT_STRIDE), :] = vals if mult is None else vals * mult

    def run_block(idx_ref, part, blk, prev_blk, acc_prev):
        acc = acc0
        for g in range(GROUP):
            cur, prev = slots[g % N_SLOT], slots[(g - 1) % N_SLOT]
            gather(idx_ref, part * GROUP + g, blk, g, cur)
            if g == 0:
                store_block(prev_blk, reduce_token(prev, prev_blk, GROUP - 1, acc_prev))
            else:
                acc = reduce_token(prev, blk, g - 1, acc)
        return acc

    @pl.when(i == 0)
    def _():
        slots[-1][...] = jnp.zeros(slots[-1].shape, F32)
        fetch(first, 0).start()
        fetch(first + 1, 1).start()

    def pair(j, acc):
        for b in range(2):
            group = first + 2 * j + b
            fetch(group, b).wait()
            for part in range(IDX_BLOCKS):
                blk = (2 * j + b) * IDX_BLOCKS + part
                acc = run_block(idx[b], part, blk, jnp.maximum(blk - 1, 0), acc)

            @pl.when(group + 2 < n_idx)
            def _():
                fetch(group + 2, b).start()

        return acc

    acc = lax.fori_loop(0, IDX_PER_TILE // 2, pair, acc0)
    last = TT // GROUP - 1
    store_block(last, reduce_token(slots[-1], last, GROUP - 1, acc))


def _peer_u_kernel(e_hbm, x_ref, u_ref, a_ref, at_ref, idx_a, idx_b, sems, *slots):
    chunk = 2 * SUBLANES
    lane = lax.broadcasted_iota(jnp.int32, (chunk, TT), 1)
    at_ref[...] = jnp.zeros(at_ref.shape, F32)

    def scale(blk, g):
        return x_ref[blk * GROUP + g]

    def reduce_token(slot, blk, row, acc):
        t = blk * GROUP + row
        for c in range(N_PAIR // chunk):
            part = slot[c * chunk:(c + 1) * chunk, :]
            for j in range(1, N_SLAB):
                part = part + slot[j * SLOT_STRIDE + c * chunk:j * SLOT_STRIDE + (c + 1) * chunk, :]
            a_col = jnp.sum(part, axis=-1, keepdims=True)
            rows = slice(c * chunk, (c + 1) * chunk)
            at_ref[rows, :] = jnp.where(lane == t, a_col, at_ref[rows, :])
        return acc

    def store_block(blk, acc):
        pass

    _stream_tile(e_hbm, u_ref, (idx_a, idx_b), sems, slots, reduce_token, store_block,
                 jnp.zeros((), jnp.int32), scale)
    a_ref[...] = at_ref[...].T


def _peer_v_kernel(e_hbm, a_ref, g_ref, v_ref, f_ref, w_ref, idx_a, idx_b, sems, *slots):
    a = a_ref[...]
    w_ref[...] = 0.5 * a * (1.0 + lax.erf(a * (2.0 ** -0.5))) * g_ref[...]

    def reduce_token(slot, blk, row, acc):
        r0 = pl.multiple_of(blk * GROUP, GROUP)
        lhs = _split_rows(w_ref[pl.ds(r0, GROUP), :], row)
        out = []
        for j in range(N_SLAB):
            slab = slot[j * SLOT_STRIDE:j * SLOT_STRIDE + N_PAIR, :].astype(BF16)
            out.append(acc[j] + _fold(jnp.dot(lhs, slab, preferred_element_type=F32)))
        return tuple(out)

    def store_block(blk, acc):
        r0 = pl.multiple_of(blk * GROUP, GROUP)
        for j in range(N_SLAB):
            f_ref[pl.ds(r0, GROUP), j * LANES:(j + 1) * LANES] = acc[j]

    _stream_tile(e_hbm, v_ref, (idx_a, idx_b), sems, slots, reduce_token, store_block,
                 tuple(jnp.zeros((GROUP, LANES), F32) for _ in range(N_SLAB)))


def _expert_scratch():
    return ([pltpu.SMEM((IDX_TOKENS * N_PAIR,), jnp.int32),
             pltpu.SMEM((IDX_TOKENS * N_PAIR,), jnp.int32),
             pltpu.SemaphoreType.DMA((2,))]
            + [pltpu.VMEM((SLOT_ROWS, LANES), F32) for _ in range(N_SLOT)])


def _peer_u(e_rows, h1, u_packed):
    n = h1.shape[0]
    return pl.pallas_call(
        _peer_u_kernel,
        grid=(n // TT,),
        in_specs=[
            pl.BlockSpec(memory_space=pl.ANY),
            pl.BlockSpec((TT, SUBLANES, LANES), lambda i: (i, 0, 0)),
            pl.BlockSpec(memory_space=pltpu.VMEM),
        ],
        out_specs=pl.BlockSpec((TT, N_PAIR), lambda i: (i, 0)),
        out_shape=jax.ShapeDtypeStruct((n, N_PAIR), F32),
        scratch_shapes=[pltpu.VMEM((N_PAIR, TT), F32)] + _expert_scratch(),
        compiler_params=_params(),
        name="peer_u",
    )(e_rows.reshape(n // IDX_TOKENS, IDX_TOKENS * N_PAIR), h1.reshape(n, SUBLANES, LANES), u_packed)


def _peer_v(e_rows, a, g, v_packed):
    n = a.shape[0]
    return pl.pallas_call(
        _peer_v_kernel,
        grid=(n // TT,),
        in_specs=[
            pl.BlockSpec(memory_space=pl.ANY),
            pl.BlockSpec((TT, N_PAIR), lambda i: (i, 0)),
            pl.BlockSpec((TT, N_PAIR), lambda i: (i, 0)),
            pl.BlockSpec(memory_space=pltpu.VMEM),
        ],
        out_specs=pl.BlockSpec((TT, D_MODEL), lambda i: (i, 0)),
        out_shape=jax.ShapeDtypeStruct((n, D_MODEL), F32),
        scratch_shapes=[pltpu.VMEM((TT, N_PAIR), F32)] + _expert_scratch(),
        compiler_params=_params(),
        name="peer_v",
    )(e_rows.reshape(n // IDX_TOKENS, IDX_TOKENS * N_PAIR), a, g, v_packed)


def _final_kernel(h_ref, f_ref, g_ref, b_ref, o_ref):
    o_ref[...] = _layer_norm(DEEPNORM_ALPHA * h_ref[...] + f_ref[...], g_ref[...], b_ref[...])


def _final(h1, f, g, b):
    n = h1.shape[0]
    return pl.pallas_call(
        _final_kernel,
        grid=(n // TM_IN,),
        in_specs=[
            pl.BlockSpec((TM_IN, D_MODEL), lambda i: (i, 0)),
            pl.BlockSpec((TM_IN, D_MODEL), lambda i: (i, 0)),
            _const_spec((1, D_MODEL)),
            _const_spec((1, D_MODEL)),
        ],
        out_specs=pl.BlockSpec((TM_IN, D_MODEL), lambda i: (i, 0)),
        out_shape=jax.ShapeDtypeStruct((n, D_MODEL), F32),
        compiler_params=_params(),
        name="final",
    )(h1, f, g, b)


def _encode(x, p):
    batch, seq_len, _ = x.shape
    assert seq_len % TQ == 0 and seq_len // GRID_W >= NAT_KH
    n = batch * seq_len
    x2 = x.reshape(n, D_MODEL)
    zp, qkv = _inproj(x2, p["eg"], p["eb"], p["w_in"], TM_IN)
    h1 = _mixer(x2, zp, qkv, p["zm_pool"], p["km"], p["vm"], p["bias"], p["pool_w"],
                p["pool_scale"], p["w_out"], p["eg"], p["eb"], p["g1"], p["b1"], batch, seq_len)
    e_rows, g = _route(h1, p["wq"], p["key1"], p["key2"])
    a = _peer_u(e_rows, h1, p["u"])
    f = _peer_v(e_rows, a, g, p["v"])
    y = _final(h1, f, p["g2"], p["b2"])
    return y.reshape(batch, seq_len, D_MODEL)


def kernel(x_prompt, x_sample, meta_tokens, emb_ln_g, emb_ln_b, w_in, pool_w, pool_scale, nat_rpb, w_out, ln1_g, ln1_b, peer_wq, peer_key1, peer_key2, peer_u, peer_v, ln2_g, ln2_b):
    row = lambda a: a.reshape(1, -1).astype(F32)
    p = {
        "eg": row(emb_ln_g), "eb": row(emb_ln_b),
        "w_in": w_in[0].astype(BF16),
        "pool_w": pool_w[0].astype(BF16),
        "pool_scale": row(pool_scale[0]),
        "bias": _attention_bias(nat_rpb[0]),
        "w_out": w_out[0].astype(BF16),
        "g1": row(ln1_g[0]), "b1": row(ln1_b[0]),
        "wq": peer_wq[0].astype(BF16),
        "key1": peer_key1[0].astype(BF16), "key2": peer_key2[0].astype(BF16),
        "u": _pack_table(peer_u[0].astype(F32)), "v": _pack_table(peer_v[0].astype(F32)),
        "g2": row(ln2_g[0]), "b2": row(ln2_b[0]),
    }
    zm_pool, qkv_m = _inproj(meta_tokens.astype(F32), p["eg"], p["eb"], p["w_in"], N_META)
    p["zm_pool"] = zm_pool
    p["km"] = qkv_m[:, NAT_WIDTH:2 * NAT_WIDTH]
    p["vm"] = qkv_m[:, 2 * NAT_WIDTH:]
    return (_encode(x_prompt, p), _encode(x_sample, p))
```

```python
import functools

import jax
import jax.numpy as jnp
from jax import lax
from jax.experimental import pallas as pl
from jax.experimental.pallas import tpu as pltpu

D_MODEL = 1024
N_META = 16
GRID_W = 64
POOL_GROUPS = 4
POOL_CH = 128
POOL_WIDTH = 512
POOL_WINDOWS = (2, 4, 8, 16)
NAT_WIDTH = 512
NAT_HEADS = 8
NAT_HEAD_DIM = 64
NAT_KH = 8
NAT_KW = 16
IN_WIDTH = 2048
PEER_HEADS = 8
PEER_NKEYS = 128
PEER_EXPERTS = PEER_NKEYS * PEER_NKEYS
PEER_TOPK = 16
PEER_DK_HALF = 128
N_PAIR = PEER_HEADS * PEER_TOPK
LN_EPS = 1e-5
NEG_INF = -1e30
DEPTH = 1
DEEPNORM_ALPHA = float((2 * DEPTH) ** 0.25)

LANES = 128
SUBLANES = 8
N_SLAB = D_MODEL // LANES
ROW_WORDS = N_SLAB // 2
VMEM_LIMIT = 56 * 1024 * 1024

TM_IN = 512
TQ = 512
TM_ROUTE = 256
TT = 128
TE = 512
GROUP = 8
IDX_BLOCKS = 2
IDX_TOKENS = IDX_BLOCKS * GROUP
IDX_PER_TILE = TT // IDX_TOKENS
N_SLOT = 4
SLOT_STRIDE = N_PAIR + 1

F32 = jnp.float32
BF16 = jnp.bfloat16


def _layer_norm(x, g, b):
    mu = jnp.mean(x, axis=-1, keepdims=True)
    xc = x - mu
    var = jnp.mean(xc * xc, axis=-1, keepdims=True)
    return xc * lax.rsqrt(var + LN_EPS) * g + b


def _dot_t(a, b):
    return lax.dot_general(a, b, (((1,), (1,)), ((), ())), preferred_element_type=F32)


def _const_spec(shape):
    nd = len(shape)
    return pl.BlockSpec(shape, lambda *_: (0,) * nd, pipeline_mode=pl.Buffered(1))


def _params(n_axes=1):
    return pltpu.CompilerParams(dimension_semantics=("arbitrary",) * n_axes,
                                vmem_limit_bytes=VMEM_LIMIT)


def _inproj_kernel(x_ref, g_ref, b_ref, w_ref, zp_ref, qkv_ref):
    h = _layer_norm(x_ref[...], g_ref[...], b_ref[...])
    z = jnp.dot(h.astype(BF16), w_ref[...], preferred_element_type=F32)
    zp_ref[...] = z[:, :POOL_WIDTH]
    qkv_ref[...] = z[:, POOL_WIDTH:].astype(BF16)


def _inproj(x, g, b, w_in_bf16, tm):
    n = x.shape[0]
    return pl.pallas_call(
        _inproj_kernel,
        grid=(n // tm,),
        in_specs=[
            pl.BlockSpec((tm, D_MODEL), lambda i: (i, 0)),
            _const_spec((1, D_MODEL)),
            _const_spec((1, D_MODEL)),
            _const_spec((D_MODEL, IN_WIDTH)),
        ],
        out_specs=[
            pl.BlockSpec((tm, POOL_WIDTH), lambda i: (i, 0)),
            pl.BlockSpec((tm, 3 * NAT_WIDTH), lambda i: (i, 0)),
        ],
        out_shape=[
            jax.ShapeDtypeStruct((n, POOL_WIDTH), F32),
            jax.ShapeDtypeStruct((n, 3 * NAT_WIDTH), BF16),
        ],
        compiler_params=_params(),
        name="inproj",
    )(x, g, b, w_in_bf16)


def _mixer_kernel(x_ref, zp_ref, zprev_ref, znext_ref, q_ref, k_ref, v_ref,
                  zmp_ref, km_ref, vm_ref, bias_ref, poolw_ref, pools_ref, wout_ref,
                  eg_ref, eb_ref, g1_ref, b1_ref, h1_ref, ext_ref, mix_ref, *, seq_len):
    i = pl.program_id(1)
    n_tiles = pl.num_programs(1)
    rows = seq_len // GRID_W
    scale = NAT_HEAD_DIM ** -0.5

    ext_ref[8:8 + TQ, :] = zp_ref[...]

    @pl.when(i == 0)
    def _():
        ext_ref[0:8, :] = zmp_ref[8:16, :]

    @pl.when(i > 0)
    def _():
        ext_ref[0:8, :] = zprev_ref[...]

    @pl.when(i == n_tiles - 1)
    def _():
        ext_ref[8 + TQ:16 + TQ, :] = jnp.zeros((8, POOL_WIDTH), F32)

    @pl.when(i < n_tiles - 1)
    def _():
        ext_ref[8 + TQ:16 + TQ, :] = znext_ref[...]

    t_seq = i * TQ + lax.broadcasted_iota(jnp.int32, (TQ, 1), 0)
    for g, w in enumerate(POOL_WINDOWS):
        cols = slice(g * POOL_CH, (g + 1) * POOL_CH)
        tot = ext_ref[8 - w // 2:8 - w // 2 + TQ, cols]
        for j in range(1, w):
            tot = tot + ext_ref[8 - w // 2 + j:8 - w // 2 + j + TQ, cols]
        over = jnp.maximum(t_seq + (w // 2 - 1) - (seq_len - 1), 0)
        count = (w - over).astype(F32)
        pooled = tot / count - ext_ref[8:8 + TQ, cols]
        y = jnp.dot(pooled.astype(BF16), poolw_ref[g], preferred_element_type=F32)
        mix_ref[:, cols] = (y * pools_ref[:, cols]).astype(BF16)

    lane = lax.broadcasted_iota(jnp.int32, (GRID_W, LANES), 1)
    first_head = lane < NAT_HEAD_DIM

    def row_body(rr, carry):
        r = i * (TQ // GRID_W) + rr
        rs = jnp.clip(r - NAT_KH // 2, 0, rows - NAT_KH)
        off = r - rs
        kstart = pl.multiple_of(rs * GRID_W, GRID_W)
        qstart = pl.multiple_of(rr * GRID_W, GRID_W)
        for p in range(NAT_HEADS // 2):
            cols = slice(p * LANES, (p + 1) * LANES)
            kp = k_ref[pl.ds(kstart, NAT_KH * GRID_W), cols]
            vp = v_ref[pl.ds(kstart, NAT_KH * GRID_W), cols]
            qp = q_ref[pl.ds(qstart, GRID_W), cols]
            kmp = km_ref[:, cols]
            vmp = vm_ref[:, cols]
            zero = jnp.zeros_like(qp)
            q2 = jnp.concatenate([jnp.where(first_head, qp, zero),
                                  jnp.where(first_head, zero, qp)], axis=0)
            s = _dot_t(q2, kp) * scale + bias_ref[off, p]
            sm = _dot_t(q2, kmp) * scale
            m = jnp.maximum(jnp.max(s, axis=-1, keepdims=True),
                            jnp.max(sm, axis=-1, keepdims=True))
            e = jnp.exp(s - m)
            em = jnp.exp(sm - m)
            den = jnp.sum(e, axis=-1, keepdims=True) + jnp.sum(em, axis=-1, keepdims=True)
            pw = (e / den).astype(BF16)
            pm = (em / den).astype(BF16)
            out = (jnp.dot(pw, vp, preferred_element_type=F32)
                   + jnp.dot(pm, vmp, preferred_element_type=F32))
            o_pair = jnp.where(first_head, out[0:GRID_W], out[GRID_W:2 * GRID_W])
            mix_ref[pl.ds(qstart, GRID_W), POOL_WIDTH + p * LANES:POOL_WIDTH + (p + 1) * LANES] = (
                o_pair.astype(BF16))
        return carry

    lax.fori_loop(0, TQ // GRID_W, row_body, 0)

    h0 = _layer_norm(x_ref[...], eg_ref[...], eb_ref[...])
    y = jnp.dot(mix_ref[...], wout_ref[...], preferred_element_type=F32)
    h1_ref[...] = _layer_norm(DEEPNORM_ALPHA * h0 + y, g1_ref[...], b1_ref[...])


def _mixer(x, zp, qkv, zm_pool, km, vm, bias, pool_w, pool_scale, w_out, eg, eb, g1, b1,
           batch, seq_len):
    n = batch * seq_len
    nt = seq_len // TQ
    hb = TQ // 8
    last_hb = n // 8 - 1
    kern = functools.partial(_mixer_kernel, seq_len=seq_len)
    return pl.pallas_call(
        kern,
        grid=(batch, nt),
        in_specs=[
            pl.BlockSpec((TQ, D_MODEL), lambda b, i: (b * nt + i, 0)),
            pl.BlockSpec((TQ, POOL_WIDTH), lambda b, i: (b * nt + i, 0)),
            pl.BlockSpec((8, POOL_WIDTH), lambda b, i: (jnp.maximum((b * nt + i) * hb - 1, 0), 0)),
            pl.BlockSpec((8, POOL_WIDTH), lambda b, i: (jnp.minimum((b * nt + i + 1) * hb, last_hb), 0)),
            pl.BlockSpec((TQ, NAT_WIDTH), lambda b, i: (b * nt + i, 0)),
            pl.BlockSpec((seq_len, NAT_WIDTH), lambda b, i: (b, 1), pipeline_mode=pl.Buffered(1)),
            pl.BlockSpec((seq_len, NAT_WIDTH), lambda b, i: (b, 2), pipeline_mode=pl.Buffered(1)),
            _const_spec((N_META, POOL_WIDTH)),
            _const_spec((N_META, NAT_WIDTH)),
            _const_spec((N_META, NAT_WIDTH)),
            _const_spec((NAT_KH, NAT_HEADS // 2, 2 * GRID_W, NAT_KH * GRID_W)),
            _const_spec((POOL_GROUPS, POOL_CH, POOL_CH)),
            _const_spec((1, POOL_WIDTH)),
            _const_spec((D_MODEL, D_MODEL)),
            _const_spec((1, D_MODEL)),
            _const_spec((1, D_MODEL)),
            _const_spec((1, D_MODEL)),
            _const_spec((1, D_MODEL)),
        ],
        out_specs=pl.BlockSpec((TQ, D_MODEL), lambda b, i: (b * nt + i, 0)),
        out_shape=jax.ShapeDtypeStruct((n, D_MODEL), F32),
        scratch_shapes=[pltpu.VMEM((TQ + 16, POOL_WIDTH), F32),
                        pltpu.VMEM((TQ, D_MODEL), BF16)],
        compiler_params=_params(2),
        name="mixer",
    )(x, zp, zp, zp, qkv, qkv, qkv, zm_pool, km, vm, bias, pool_w, pool_scale, w_out,
      eg, eb, g1, b1)


def _attention_bias(rpb):
    c = jnp.arange(GRID_W)
    c_start = jnp.clip(c - NAT_KW // 2, 0, GRID_W - NAT_KW)
    col_valid = (c[None, :] >= c_start[:, None]) & (c[None, :] < c_start[:, None] + NAT_KW)
    col_off = jnp.clip(c[None, :] - c[:, None] + (NAT_KW - 1), 0, 2 * NAT_KW - 2)
    pick = (col_off[:, :, None] == jnp.arange(2 * NAT_KW - 1)).astype(F32)
    toep = jnp.einsum("qkc,hdc->hdqk", pick, rpb.astype(F32), precision=lax.Precision.HIGHEST)
    toep = jnp.where(col_valid[None, None], toep, NEG_INF)
    b = jnp.stack([toep[:, NAT_KH - 1 - off:2 * NAT_KH - 1 - off] for off in range(NAT_KH)])
    b = b.transpose(0, 1, 3, 2, 4)
    return b.reshape(NAT_KH, NAT_HEADS // 2, 2 * GRID_W, NAT_KH * GRID_W)


def _top_rows(s, k, payload=None):
    n_rows = s.shape[0]
    row = lax.broadcasted_iota(jnp.int32, s.shape, 0).astype(F32)
    vals, picks = [], []
    for _ in range(k):
        m = jnp.max(s, axis=0, keepdims=True)
        idx = jnp.min(jnp.where(s == m, row, float(n_rows)), axis=0, keepdims=True)
        hit = row == idx
        vals.append(m)
        picks.append(idx if payload is None
                     else jnp.max(jnp.where(hit, payload, -1.0), axis=0, keepdims=True))
        s = jnp.where(hit, -jnp.inf, s)
    return jnp.concatenate(vals, axis=0), jnp.concatenate(picks, axis=0)


def _candidates(first, second, combine):
    parts = [combine(first[0:1, :], second)]
    parts += [combine(first[a:a + 1, :], second[0:SUBLANES, :]) for a in range(1, PEER_TOPK)]
    return jnp.concatenate(parts, axis=0)


def _route_kernel(h_ref, wq_ref, k1_ref, k2_ref, e_ref, g_ref):
    q = jnp.dot(h_ref[...].astype(BF16), wq_ref[...], preferred_element_type=F32)
    q = q.astype(BF16)
    n_cand = PEER_TOPK + SUBLANES * (PEER_TOPK - 1)
    r = lax.broadcasted_iota(jnp.int32, (n_cand, LANES), 0)
    ca = jnp.where(r < PEER_TOPK, 0, ((r - PEER_TOPK) >> 3) + 1)
    cb = jnp.where(r < PEER_TOPK, r, (r - PEER_TOPK) & (SUBLANES - 1))
    reachable = (ca + 1) * (cb + 1) <= PEER_TOPK
    for c in range(TM_ROUTE // LANES):
        tok = slice(c * LANES, (c + 1) * LANES)
        e_rows, g_rows = [], []
        for h in range(PEER_HEADS):
            q1 = q[tok, (2 * h) * PEER_DK_HALF:(2 * h + 1) * PEER_DK_HALF]
            q2 = q[tok, (2 * h + 1) * PEER_DK_HALF:(2 * h + 2) * PEER_DK_HALF]
            t1, i1 = _top_rows(_dot_t(k1_ref[h], q1), PEER_TOPK)
            t2, i2 = _top_rows(_dot_t(k2_ref[h], q2), PEER_TOPK)
            cand = jnp.where(reachable, _candidates(t1, t2, lambda x, y: x + y), -jnp.inf)
            expert = _candidates(i1, i2, lambda x, y: x * PEER_NKEYS + y)
            top_s, top_e = _top_rows(cand, PEER_TOPK, payload=expert)
            e_rows.append(top_e)
            ex = jnp.exp(top_s - top_s[0:1, :])
            g_rows.append(ex / jnp.sum(ex, axis=0, keepdims=True))
        e_all = jnp.concatenate(e_rows, axis=0).astype(jnp.int32)
        e_ref[tok, :] = (e_all * ROW_WORDS).T
        g_ref[tok, :] = jnp.concatenate(g_rows, axis=0).T


def _route(h1, wq_bf16, key1_bf16, key2_bf16):
    n = h1.shape[0]
    return pl.pallas_call(
        _route_kernel,
        grid=(n // TM_ROUTE,),
        in_specs=[
            pl.BlockSpec((TM_ROUTE, D_MODEL), lambda i: (i, 0)),
            _const_spec((D_MODEL, 2 * PEER_HEADS * PEER_DK_HALF)),
            _const_spec((PEER_HEADS, PEER_NKEYS, PEER_DK_HALF)),
            _const_spec((PEER_HEADS, PEER_NKEYS, PEER_DK_HALF)),
        ],
        out_specs=[
            pl.BlockSpec((TM_ROUTE, N_PAIR), lambda i: (i, 0)),
            pl.BlockSpec((TM_ROUTE, N_PAIR), lambda i: (i, 0)),
        ],
        out_shape=[
            jax.ShapeDtypeStruct((n, N_PAIR), jnp.int32),
            jax.ShapeDtypeStruct((n, N_PAIR), F32),
        ],
        compiler_params=_params(),
        name="route",
    )(h1, wq_bf16, key1_bf16, key2_bf16)


def _pack_kernel(t_ref, o_ref):
    o_ref[...] = pltpu.bitcast(t_ref[...].astype(BF16), jnp.int32)


def _pack_table(tab):
    out = pl.pallas_call(
        _pack_kernel,
        grid=(PEER_EXPERTS // TE,),
        in_specs=[pl.BlockSpec((TE, N_SLAB, LANES), lambda i: (i, 0, 0))],
        out_specs=pl.BlockSpec((TE, ROW_WORDS, LANES), lambda i: (i, 0, 0)),
        out_shape=jax.ShapeDtypeStruct((PEER_EXPERTS, ROW_WORDS, LANES), jnp.int32),
        compiler_params=_params(),
        name="pack_table",
    )(tab.reshape(PEER_EXPERTS, N_SLAB, LANES))
    return out.reshape(PEER_EXPERTS * ROW_WORDS, LANES)


def _fold(r):
    return r[0:SUBLANES] + r[SUBLANES:2 * SUBLANES]


def _stream_tile(e_hbm, tab_ref, idx, sems, slots, token_ctx, put_row, reduce_token, store_block,
                 acc0):
    i = pl.program_id(0)
    first = i * IDX_PER_TILE
    n_idx = pl.num_programs(0) * IDX_PER_TILE

    def fetch(group, b):
        return pltpu.make_async_copy(e_hbm.at[group], idx[b], sems.at[b])

    def gather(idx_ref, tok, blk, g, slot):
        ctx = token_ctx(blk, g)
        for k in range(N_PAIR):
            row = pl.multiple_of(idx_ref[tok * N_PAIR + k], ROW_WORDS)
            put_row(slot, k, tab_ref[pl.ds(row, ROW_WORDS), :], ctx)

    def run_block(idx_ref, part, blk, prev_blk, acc_prev):
        acc = acc0
        for g in range(GROUP):
            cur, prev = slots[g % N_SLOT], slots[(g - 1) % N_SLOT]
            gather(idx_ref, part * GROUP + g, blk, g, cur)
            if g == 0:
                store_block(prev_blk, reduce_token(prev, prev_blk, GROUP - 1, acc_prev))
            else:
                acc = reduce_token(prev, blk, g - 1, acc)
        return acc

    @pl.when(i == 0)
    def _():
        slots[-1][...] = jnp.zeros(slots[-1].shape, slots[-1].dtype)
        fetch(first, 0).start()
        fetch(first + 1, 1).start()

    def pair(j, acc):
        for b in range(2):
            group = first + 2 * j + b
            fetch(group, b).wait()
            for part in range(IDX_BLOCKS):
                blk = (2 * j + b) * IDX_BLOCKS + part
                acc = run_block(idx[b], part, blk, jnp.maximum(blk - 1, 0), acc)

            @pl.when(group + 2 < n_idx)
            def _():
                fetch(group + 2, b).start()

        return acc

    acc = lax.fori_loop(0, IDX_PER_TILE // 2, pair, acc0)
    last = TT // GROUP - 1
    store_block(last, reduce_token(slots[-1], last, GROUP - 1, acc))


def _peer_u_kernel(e_hbm, x_ref, u_ref, a_ref, at_ref, idx_a, idx_b, sems, *slots):
    chunk = 2 * SUBLANES
    lane = lax.broadcasted_iota(jnp.int32, (chunk, TT), 1)
    at_ref[...] = jnp.zeros(at_ref.shape, F32)

    def token_ctx(blk, g):
        return x_ref[blk * GROUP + g]

    def put_row(slot, k, words, xt):
        slot[pl.ds(k, SUBLANES, stride=SLOT_STRIDE), :] = pltpu.bitcast(words, BF16).astype(F32) * xt

    def reduce_token(slot, blk, row, acc):
        t = blk * GROUP + row
        for c in range(N_PAIR // chunk):
            part = slot[c * chunk:(c + 1) * chunk, :]
            for j in range(1, N_SLAB):
                part = part + slot[j * SLOT_STRIDE + c * chunk:j * SLOT_STRIDE + (c + 1) * chunk, :]
            a_col = jnp.sum(part, axis=-1, keepdims=True)
            rows = slice(c * chunk, (c + 1) * chunk)
            at_ref[rows, :] = jnp.where(lane == t, a_col, at_ref[rows, :])
        return acc

    def store_block(blk, acc):
        pass

    _stream_tile(e_hbm, u_ref, (idx_a, idx_b), sems, slots, token_ctx, put_row, reduce_token,
                 store_block, jnp.zeros((), jnp.int32))
    a_ref[...] = at_ref[...].T


def _peer_v_kernel(e_hbm, a_ref, g_ref, v_ref, f_ref, w4_ref, idx_a, idx_b, sems, *slots):
    a = a_ref[...]
    w = 0.5 * a * (1.0 + lax.erf(a * (2.0 ** -0.5))) * g_ref[...]
    hi = w.astype(BF16)
    lo = (w - hi.astype(F32)).astype(BF16)
    src = lax.broadcasted_iota(jnp.int32, (N_PAIR, 2 * N_PAIR), 0)
    dst = lax.broadcasted_iota(jnp.int32, (N_PAIR, 2 * N_PAIR), 1)
    for par in range(2):
        spread = (dst == 2 * src + par).astype(BF16)
        w4_ref[2 * par] = jnp.dot(hi, spread, preferred_element_type=F32)
        w4_ref[2 * par + 1] = jnp.dot(lo, spread, preferred_element_type=F32)

    def token_ctx(blk, g):
        return None

    def put_row(slot, k, words, ctx):
        slot[pl.ds(k, ROW_WORDS, stride=SLOT_STRIDE), :] = words

    def reduce_token(slot, blk, row, acc):
        r0 = pl.multiple_of(blk * GROUP, GROUP)
        sub = lax.broadcasted_iota(jnp.int32, (GROUP, 2 * N_PAIR), 0)
        lhs = jnp.concatenate(
            [jnp.where(sub == row, w4_ref[q, pl.ds(r0, GROUP), :], 0.0).astype(BF16)
             for q in range(4)], axis=0)
        out = list(acc)
        for s in range(ROW_WORDS):
            slab = pltpu.bitcast(slot[s * SLOT_STRIDE:s * SLOT_STRIDE + N_PAIR, :], BF16)
            r = jnp.dot(lhs, slab, preferred_element_type=F32)
            out[2 * s] = acc[2 * s] + _fold(r[0:2 * GROUP])
            out[2 * s + 1] = acc[2 * s + 1] + _fold(r[2 * GROUP:4 * GROUP])
        return tuple(out)

    def store_block(blk, acc):
        r0 = pl.multiple_of(blk * GROUP, GROUP)
        for j in range(N_SLAB):
            f_ref[pl.ds(r0, GROUP), j * LANES:(j + 1) * LANES] = acc[j]

    _stream_tile(e_hbm, v_ref, (idx_a, idx_b), sems, slots, token_ctx, put_row, reduce_token,
                 store_block, tuple(jnp.zeros((GROUP, LANES), F32) for _ in range(N_SLAB)))


def _expert_scratch(slot_sublanes, slot_dtype):
    slot_rows = slot_sublanes * SLOT_STRIDE + SUBLANES
    return ([pltpu.SMEM((IDX_TOKENS * N_PAIR,), jnp.int32),
             pltpu.SMEM((IDX_TOKENS * N_PAIR,), jnp.int32),
             pltpu.SemaphoreType.DMA((2,))]
            + [pltpu.VMEM((slot_rows, LANES), slot_dtype) for _ in range(N_SLOT)])


def _peer_u(e_rows, h1, u_packed):
    n = h1.shape[0]
    return pl.pallas_call(
        _peer_u_kernel,
        grid=(n // TT,),
        in_specs=[
            pl.BlockSpec(memory_space=pl.ANY),
            pl.BlockSpec((TT, SUBLANES, LANES), lambda i: (i, 0, 0)),
            pl.BlockSpec(memory_space=pltpu.VMEM),
        ],
        out_specs=pl.BlockSpec((TT, N_PAIR), lambda i: (i, 0)),
        out_shape=jax.ShapeDtypeStruct((n, N_PAIR), F32),
        scratch_shapes=[pltpu.VMEM((N_PAIR, TT), F32)] + _expert_scratch(N_SLAB, F32),
        compiler_params=_params(),
        name="peer_u",
    )(e_rows.reshape(n // IDX_TOKENS, IDX_TOKENS * N_PAIR), h1.reshape(n, SUBLANES, LANES), u_packed)


def _peer_v(e_rows, a, g, v_packed):
    n = a.shape[0]
    return pl.pallas_call(
        _peer_v_kernel,
        grid=(n // TT,),
        in_specs=[
            pl.BlockSpec(memory_space=pl.ANY),
            pl.BlockSpec((TT, N_PAIR), lambda i: (i, 0)),
            pl.BlockSpec((TT, N_PAIR), lambda i: (i, 0)),
            pl.BlockSpec(memory_space=pltpu.VMEM),
        ],
        out_specs=pl.BlockSpec((TT, D_MODEL), lambda i: (i, 0)),
        out_shape=jax.ShapeDtypeStruct((n, D_MODEL), F32),
        scratch_shapes=([pltpu.VMEM((4, TT, 2 * N_PAIR), F32)]
                        + _expert_scratch(ROW_WORDS, jnp.int32)),
        compiler_params=_params(),
        name="peer_v",
    )(e_rows.reshape(n // IDX_TOKENS, IDX_TOKENS * N_PAIR), a, g, v_packed)


def _final_kernel(h_ref, f_ref, g_ref, b_ref, o_ref):
    o_ref[...] = _layer_norm(DEEPNORM_ALPHA * h_ref[...] + f_ref[...], g_ref[...], b_ref[...])


def _final(h1, f, g, b):
    n = h1.shape[0]
    return pl.pallas_call(
        _final_kernel,
        grid=(n // TM_IN,),
        in_specs=[
            pl.BlockSpec((TM_IN, D_MODEL), lambda i: (i, 0)),
            pl.BlockSpec((TM_IN, D_MODEL), lambda i: (i, 0)),
            _const_spec((1, D_MODEL)),
            _const_spec((1, D_MODEL)),
        ],
        out_specs=pl.BlockSpec((TM_IN, D_MODEL), lambda i: (i, 0)),
        out_shape=jax.ShapeDtypeStruct((n, D_MODEL), F32),
        compiler_params=_params(),
        name="final",
    )(h1, f, g, b)


def _encode(x, p):
    batch, seq_len, _ = x.shape
    assert seq_len % TQ == 0 and seq_len // GRID_W >= NAT_KH
    n = batch * seq_len
    x2 = x.reshape(n, D_MODEL)
    zp, qkv = _inproj(x2, p["eg"], p["eb"], p["w_in"], TM_IN)
    h1 = _mixer(x2, zp, qkv, p["zm_pool"], p["km"], p["vm"], p["bias"], p["pool_w"],
                p["pool_scale"], p["w_out"], p["eg"], p["eb"], p["g1"], p["b1"], batch, seq_len)
    e_rows, g = _route(h1, p["wq"], p["key1"], p["key2"])
    a = _peer_u(e_rows, h1, p["u"])
    f = _peer_v(e_rows, a, g, p["v"])
    y = _final(h1, f, p["g2"], p["b2"])
    return y.reshape(batch, seq_len, D_MODEL)


def kernel(x_prompt, x_sample, meta_tokens, emb_ln_g, emb_ln_b, w_in, pool_w, pool_scale, nat_rpb, w_out, ln1_g, ln1_b, peer_wq, peer_key1, peer_key2, peer_u, peer_v, ln2_g, ln2_b):
    row = lambda a: a.reshape(1, -1).astype(F32)
    p = {
        "eg": row(emb_ln_g), "eb": row(emb_ln_b),
        "w_in": w_in[0].astype(BF16),
        "pool_w": pool_w[0].astype(BF16),
        "pool_scale": row(pool_scale[0]),
        "bias": _attention_bias(nat_rpb[0]),
        "w_out": w_out[0].astype(BF16),
        "g1": row(ln1_g[0]), "b1": row(ln1_b[0]),
        "wq": peer_wq[0].astype(BF16),
        "key1": peer_key1[0].astype(BF16), "key2": peer_key2[0].astype(BF16),
        "u": _pack_table(peer_u[0].astype(F32)), "v": _pack_table(peer_v[0].astype(F32)),
        "g2": row(ln2_g[0]), "b2": row(ln2_b[0]),
    }
    zm_pool, qkv_m = _inproj(meta_tokens.astype(F32), p["eg"], p["eb"], p["w_in"], N_META)
    p["zm_pool"] = zm_pool
    p["km"] = qkv_m[:, NAT_WIDTH:2 * NAT_WIDTH]
    p["vm"] = qkv_m[:, 2 * NAT_WIDTH:]
    return (_encode(x_prompt, p), _encode(x_sample, p))
```
